```python
import jax, jax.numpy as jnp
from jax import lax
import numpy as np

D_MODEL = 1024
BATCH = 4
SEQ = 4096
DEPTH = 2
DEC_BATCH = 32
DEC_SEQ = 4
PAST_LEN = 16384
PAGE_SIZE = 128

N_MIXERS = 2
N_ATTN_LAYERS = (DEPTH + 1) // 2
N_RET_LAYERS = DEPTH // 2
N_HEADS = 16
HEAD_DIM = D_MODEL // N_HEADS
MOBA_BLOCK = 256
MOBA_TOPK = 3
Q_CHUNK = 32
R_HEADS = 4
R_DK = D_MODEL // R_HEADS
R_DV = 2 * R_DK
R_Q = R_HEADS * R_DK
R_V = R_HEADS * R_DV
R_IN = 2 * R_Q + 2 * R_V
R_CHUNK = 128
PEER_HEADS = 8
PEER_NKEYS = 128
PEER_EXPERTS = PEER_NKEYS * PEER_NKEYS
PEER_DKEY = 256
PEER_HALF = PEER_DKEY // 2
PEER_TOPK = 16
PEER_TOK_CHUNK = 128
DN_ALPHA = (2 * DEPTH) ** 0.25
DN_BETA = (8 * DEPTH) ** -0.25
LN_EPS = 1e-5

kernel_name = "moba_retnet_peer_hybrid_step"


def _layer_norm(x, g, b):
    xf = x.astype(jnp.float32)
    mu = jnp.mean(xf, -1, keepdims=True)
    var = jnp.mean(jnp.square(xf - mu), -1, keepdims=True)
    y = (xf - mu) * lax.rsqrt(var + LN_EPS) * g.astype(jnp.float32) + b.astype(jnp.float32)
    return y.astype(x.dtype)


def _ada(c, w, b):
    m = jax.nn.silu(c) @ w + b
    return jnp.split(m, 6, axis=-1)


def _modulate(x, shift, scale):
    return x * (1.0 + scale[:, None, :]) + shift[:, None, :]


def _alibi_slopes():
    return 2.0 ** (-8.0 * jnp.arange(1, N_HEADS + 1, dtype=jnp.float32) / N_HEADS)


def _moba_qkv(hx, w):
    b, t, _ = hx.shape
    z = (hx @ w).reshape(b, t, 3, N_HEADS, HEAD_DIM)
    return z[:, :, 0], z[:, :, 1], z[:, :, 2]


def _select_blocks(q, k_mean, n_full):
    nb = k_mean.shape[2]
    s = jnp.einsum('bhqd,bhnd->bhqn', q, k_mean).astype(jnp.float32)
    valid = jnp.arange(nb)[None, :] < n_full[:, None]
    s = jnp.where(valid, s, -jnp.inf)
    if nb < MOBA_TOPK:
        s = jnp.pad(s, ((0, 0), (0, 0), (0, 0), (0, MOBA_TOPK - nb)), constant_values=-jnp.inf)
    _, sel = lax.top_k(s, MOBA_TOPK)
    sel = jnp.minimum(sel, max(nb - 1, 0)).astype(jnp.int32)
    slot_ok = jnp.arange(MOBA_TOPK)[None, :] < n_full[:, None]
    return sel, slot_ok


def _moba_attend(q, t_q, k_sel, v_sel, pos_sel, ok_sel, k_own, v_own, pos_own, ok_own):
    slopes = _alibi_slopes()
    scale = HEAD_DIM ** -0.5
    b, h, nq = q.shape[:3]
    dist_sel = (t_q[:, None, None] - pos_sel).astype(jnp.float32)
    s_sel = jnp.einsum('bhqd,bhqsld->bhqsl', q, k_sel).astype(jnp.float32) * scale - slopes[:, None, None, None] * dist_sel
    s_sel = jnp.where(ok_sel, s_sel, -jnp.inf).reshape(b, h, nq, -1)
    dist_own = (t_q[:, None] - pos_own[None, :]).astype(jnp.float32)
    s_own = jnp.einsum('bhqd,bhld->bhql', q, k_own).astype(jnp.float32) * scale - slopes[:, None, None] * dist_own
    s_own = jnp.where(ok_own, s_own, -jnp.inf)
    n_sel = s_sel.shape[-1]
    p = jax.nn.softmax(jnp.concatenate([s_sel, s_own], axis=-1), axis=-1).astype(v_own.dtype)
    p_sel = p[..., :n_sel].reshape(k_sel.shape[:-1])
    return (jnp.einsum('bhqsl,bhqsld->bhqd', p_sel, v_sel)
            + jnp.einsum('bhql,bhld->bhqd', p[..., n_sel:], v_own))


def _moba_prompt(q, k, v):
    b, t, h, dh = q.shape
    nb = -(-t // MOBA_BLOCK)
    pad = nb * MOBA_BLOCK - t

    def blocks(a):
        a = jnp.pad(a, ((0, 0), (0, pad), (0, 0), (0, 0)))
        return a.reshape(b, nb, MOBA_BLOCK, h, dh).transpose(0, 3, 1, 2, 4)

    kb, vb = blocks(k), blocks(v)
    k_mean = jnp.mean(kb.astype(jnp.float32), axis=3).astype(k.dtype)
    qh = q.transpose(0, 2, 1, 3)
    pos = jnp.arange(t, dtype=jnp.int32)
    sel, slot_ok = _select_blocks(qh, k_mean, pos // MOBA_BLOCK)
    n_chunks = t // Q_CHUNK
    q_c = qh.reshape(b, h, n_chunks, Q_CHUNK, dh).transpose(2, 0, 1, 3, 4)
    sel_c = sel.reshape(b, h, n_chunks, Q_CHUNK, MOBA_TOPK).transpose(2, 0, 1, 3, 4)
    ok_c = slot_ok.reshape(n_chunks, Q_CHUNK, MOBA_TOPK)
    bi = jnp.arange(b)[:, None, None, None]
    hi = jnp.arange(h)[None, :, None, None]
    offs = jnp.arange(MOBA_BLOCK, dtype=jnp.int32)

    def chunk(args):
        ci, qq, ss, ok = args
        t_q = ci * Q_CHUNK + jnp.arange(Q_CHUNK, dtype=jnp.int32)
        k_sel = kb[bi, hi, ss]
        v_sel = vb[bi, hi, ss]
        pos_sel = ss[..., None] * MOBA_BLOCK + offs
        own = (ci * Q_CHUNK) // MOBA_BLOCK
        k_own = lax.dynamic_index_in_dim(kb, own, axis=2, keepdims=False)
        v_own = lax.dynamic_index_in_dim(vb, own, axis=2, keepdims=False)
        pos_own = own * MOBA_BLOCK + offs
        ok_own = pos_own[None, :] <= t_q[:, None]
        return _moba_attend(qq, t_q, k_sel, v_sel, pos_sel, ok[None, None, :, :, None],
                            k_own, v_own, pos_own, ok_own)

    out = lax.map(chunk, (jnp.arange(n_chunks, dtype=jnp.int32), q_c, sel_c, ok_c))
    return out.transpose(1, 0, 3, 2, 4).reshape(b, t, h, dh)


def _moba_sample(q, k, v, cache_k, cache_v, layer, page_table):
    db, tn, h, dh = q.shape
    n_pages = page_table.shape[1]
    past_len = n_pages * PAGE_SIZE
    ppb = MOBA_BLOCK // PAGE_SIZE
    nbp = past_len // MOBA_BLOCK
    own_start = nbp * MOBA_BLOCK
    t_q = past_len + jnp.arange(tn, dtype=jnp.int32)

    def seq_block_means(pt_row):
        rows = cache_k[layer, pt_row[:nbp * ppb]]
        rows = rows.reshape(nbp, MOBA_BLOCK, h, dh).astype(jnp.float32)
        return jnp.mean(rows, axis=1).astype(q.dtype)

    k_mean = lax.map(seq_block_means, page_table).transpose(0, 2, 1, 3)
    qh = q.transpose(0, 2, 1, 3)
    sel, slot_ok = _select_blocks(qh, k_mean, t_q // MOBA_BLOCK)
    bi = jnp.arange(db)[:, None, None, None, None]
    hi = jnp.arange(h)[None, :, None, None, None]
    page_idx = jnp.clip(sel[..., None] * ppb + jnp.arange(ppb), 0, n_pages - 1)
    phys = page_table[bi, page_idx]
    sel_shape = sel.shape + (MOBA_BLOCK, dh)
    k_sel = cache_k[layer, phys, :, hi].reshape(sel_shape)
    v_sel = cache_v[layer, phys, :, hi].reshape(sel_shape)
    pos_sel = sel[..., None] * MOBA_BLOCK + jnp.arange(MOBA_BLOCK, dtype=jnp.int32)
    n_own_pages = n_pages - own_start // PAGE_SIZE
    own_pt = page_table[:, own_start // PAGE_SIZE:]
    k_own_c = cache_k[layer, own_pt].reshape(db, n_own_pages * PAGE_SIZE, h, dh)
    v_own_c = cache_v[layer, own_pt].reshape(db, n_own_pages * PAGE_SIZE, h, dh)
    k_own = jnp.concatenate([k_own_c.astype(k.dtype), k], axis=1).transpose(0, 2, 1, 3)
    v_own = jnp.concatenate([v_own_c.astype(v.dtype), v], axis=1).transpose(0, 2, 1, 3)
    pos_own = jnp.arange(own_start, past_len + tn, dtype=jnp.int32)
    ok_own = (pos_own[None, :] <= t_q[:, None]) & (pos_own[None, :] >= (t_q[:, None] // MOBA_BLOCK) * MOBA_BLOCK)
    out = _moba_attend(qh, t_q, k_sel, v_sel, pos_sel, slot_ok[None, None, :, :, None],
                       k_own, v_own, pos_own, ok_own)
    return out.transpose(0, 2, 1, 3)


def _retention_log_gamma():
    return jnp.log(1.0 - 2.0 ** (-5.0 - jnp.arange(R_HEADS, dtype=jnp.float32)))


def _retention_qkvg(hx, w_in):
    b, t, _ = hx.shape
    z = hx @ w_in
    q, k, v, g = jnp.split(z, [R_Q, 2 * R_Q, 2 * R_Q + R_V], axis=-1)

    def heads(a, d):
        return a.reshape(b, t, R_HEADS, d).transpose(0, 2, 1, 3).astype(jnp.float32)

    return heads(q, R_DK), heads(k, R_DK) * (R_DK ** -0.5), heads(v, R_DV), g


def _retention_chunk(q, k, v, s0, log_gamma):
    c = q.shape[2]
    idx = jnp.arange(c, dtype=jnp.float32)
    diff = idx[:, None] - idx[None, :]
    decay = jnp.where(diff >= 0, jnp.exp(log_gamma[:, None, None] * jnp.maximum(diff, 0.0)), 0.0)
    inner = jnp.einsum('bhid,bhjd->bhij', q, k) * decay
    o = jnp.einsum('bhij,bhjv->bhiv', inner, v)
    q_dec = jnp.exp(log_gamma[:, None] * (idx + 1.0))
    o = o + jnp.einsum('bhid,bhdv->bhiv', q * q_dec[:, :, None], s0)
    k_dec = jnp.exp(log_gamma[:, None] * (c - 1.0 - idx))
    s1 = s0 * jnp.exp(log_gamma * c)[:, None, None] + jnp.einsum('bhjd,bhjv->bhdv', k * k_dec[:, :, None], v)
    return o, s1


def _retention_prompt(q, k, v, log_gamma):
    b, h, t, _ = q.shape
    nc = t // R_CHUNK

    def resh(a):
        return a.reshape(b, h, nc, R_CHUNK, a.shape[-1]).transpose(2, 0, 1, 3, 4)

    def step(s, xs):
        qc, kc, vc = xs
        o, s = _retention_chunk(qc, kc, vc, s, log_gamma)
        return s, o

    s0 = jnp.zeros((b, h, R_DK, R_DV), jnp.float32)
    s_fin, o = lax.scan(step, s0, (resh(q), resh(k), resh(v)))
    return o.transpose(1, 2, 0, 3, 4).reshape(b, h, t, R_DV), s_fin


def _retention_out(o, g, w_o):
    b, h, t, dv = o.shape
    mu = jnp.mean(o, -1, keepdims=True)
    var = jnp.mean(jnp.square(o - mu), -1, keepdims=True)
    on = ((o - mu) * lax.rsqrt(var + LN_EPS)).transpose(0, 2, 1, 3).reshape(b, t, h * dv).astype(g.dtype)
    return (jax.nn.silu(g) * on) @ w_o


def _peer(hx, w_q, sub_keys, u_tab, v_tab):
    shp = hx.shape
    xt = hx.reshape(-1, shp[-1])
    n = xt.shape[0]
    n_pad = (-n) % PEER_TOK_CHUNK
    xc = jnp.pad(xt, ((0, n_pad), (0, 0))).reshape(-1, PEER_TOK_CHUNK, shp[-1])

    def chunk(xx):
        q = (xx @ w_q).reshape(PEER_TOK_CHUNK, PEER_HEADS, 2, PEER_HALF)
        s = jnp.einsum('nhpd,hpkd->nhpk', q, sub_keys).astype(jnp.float32)
        sc, ix = lax.top_k(s, PEER_TOPK)
        cand = (sc[:, :, 0, :, None] + sc[:, :, 1, None, :]).reshape(PEER_TOK_CHUNK, PEER_HEADS, PEER_TOPK * PEER_TOPK)
        top_sc, ci = lax.top_k(cand, PEER_TOPK)
        e = (jnp.take_along_axis(ix[:, :, 0], ci // PEER_TOPK, axis=-1) * PEER_NKEYS
             + jnp.take_along_axis(ix[:, :, 1], ci % PEER_TOPK, axis=-1))
        g = jax.nn.softmax(top_sc, axis=-1)
        a = jax.nn.gelu(jnp.einsum('nd,nhkd->nhk', xx, u_tab[e]).astype(jnp.float32), approximate=False)
        return jnp.einsum('nhk,nhkd->nd', (g * a).astype(xx.dtype), v_tab[e])

    y = lax.map(chunk, xc).reshape(-1, shp[-1])[:n]
    return y.reshape(shp)


def setup_inputs(seed: int = 0) -> dict:
    key = jax.random.key(seed)
    ks = jax.random.split(key, 20)
    f32 = jnp.float32
    n_pages = PAST_LEN // PAGE_SIZE
    n_used = DEC_BATCH * n_pages
    n_pool = n_used + n_used // 4

    def nrm(k, shape, scale):
        return jax.random.normal(k, shape, f32) * scale

    x_prompt = nrm(ks[0], (BATCH, SEQ, D_MODEL), 1.0)
    x_sample = nrm(ks[1], (DEC_BATCH, DEC_SEQ, D_MODEL), 1.0)
    cache_k = nrm(ks[2], (N_ATTN_LAYERS, n_pool, PAGE_SIZE, N_HEADS, HEAD_DIM), 1.0)
    cache_v = nrm(ks[3], (N_ATTN_LAYERS, n_pool, PAGE_SIZE, N_HEADS, HEAD_DIM), 1.0)
    state_ret = nrm(ks[4], (N_RET_LAYERS, DEC_BATCH, R_HEADS, R_DK, R_DV), 1.0)
    page_table = jax.random.permutation(ks[5], n_pool)[:n_used].reshape(DEC_BATCH, n_pages).astype(jnp.int32)
    c_prompt = nrm(ks[6], (BATCH, D_MODEL), 1.0)
    c_sample = nrm(ks[7], (DEC_BATCH, D_MODEL), 1.0)
    w_ada = nrm(ks[8], (DEPTH, D_MODEL, 6 * D_MODEL), D_MODEL ** -0.5)
    b_ada = nrm(ks[9], (DEPTH, 6 * D_MODEL), 0.02)
    ln_g = 1.0 + nrm(ks[10], (DEPTH, 2, D_MODEL), 0.02)
    ln_b = nrm(ks[11], (DEPTH, 2, D_MODEL), 0.02)
    w_qkv_attn = nrm(ks[12], (N_ATTN_LAYERS, D_MODEL, 3 * D_MODEL), D_MODEL ** -0.5)
    w_qkv_attn = w_qkv_attn.at[:, :, 2 * D_MODEL:].multiply(DN_BETA)
    w_o_attn = nrm(ks[13], (N_ATTN_LAYERS, D_MODEL, D_MODEL), D_MODEL ** -0.5 * DN_BETA)
    w_in_ret = nrm(ks[14], (N_RET_LAYERS, D_MODEL, R_IN), D_MODEL ** -0.5)
    w_in_ret = w_in_ret.at[:, :, 2 * R_Q:2 * R_Q + R_V].multiply(DN_BETA)
    w_o_ret = nrm(ks[15], (N_RET_LAYERS, R_V, D_MODEL), R_V ** -0.5 * DN_BETA)
    w_q_peer = nrm(ks[16], (DEPTH, D_MODEL, PEER_HEADS * PEER_DKEY), D_MODEL ** -0.5)
    keys_peer = nrm(ks[17], (DEPTH, PEER_HEADS, 2, PEER_NKEYS, PEER_HALF), PEER_HALF ** -0.5)
    u_peer = nrm(ks[18], (DEPTH, PEER_EXPERTS, D_MODEL), D_MODEL ** -0.5)
    v_peer = nrm(ks[19], (DEPTH, PEER_EXPERTS, D_MODEL), (PEER_HEADS * PEER_TOPK) ** -0.5 * DN_BETA)
    return {"x_prompt": x_prompt, "x_sample": x_sample, "cache_k": cache_k, "cache_v": cache_v,
            "state_ret": state_ret, "page_table": page_table, "c_prompt": c_prompt, "c_sample": c_sample,
            "w_ada": w_ada, "b_ada": b_ada, "ln_g": ln_g, "ln_b": ln_b,
            "w_qkv_attn": w_qkv_attn, "w_o_attn": w_o_attn, "w_in_ret": w_in_ret, "w_o_ret": w_o_ret,
            "w_q_peer": w_q_peer, "keys_peer": keys_peer, "u_peer": u_peer, "v_peer": v_peer}


def reference(x_prompt, x_sample, cache_k, cache_v, state_ret, page_table, c_prompt, c_sample,
              w_ada, b_ada, ln_g, ln_b, w_qkv_attn, w_o_attn, w_in_ret, w_o_ret,
              w_q_peer, keys_peer, u_peer, v_peer):
    log_gamma = _retention_log_gamma()
    yp, ys = x_prompt, x_sample
    bp, tp, _ = yp.shape
    bs, ts, _ = ys.shape
    kp_l, vp_l, ks_l, vs_l, sp_l, ss_l = [], [], [], [], [], []
    for i in range(DEPTH):
        mp = _ada(c_prompt, w_ada[i], b_ada[i])
        ms = _ada(c_sample, w_ada[i], b_ada[i])
        hp = _modulate(yp, mp[0], mp[1])
        hs = _modulate(ys, ms[0], ms[1])
        j = i // N_MIXERS
        if i % N_MIXERS == 0:
            qp, kp, vp = _moba_qkv(hp, w_qkv_attn[j])
            ap = _moba_prompt(qp, kp, vp).reshape(bp, tp, D_MODEL) @ w_o_attn[j]
            qs, kss, vss = _moba_qkv(hs, w_qkv_attn[j])
            a_s = _moba_sample(qs, kss, vss, cache_k, cache_v, j, page_table).reshape(bs, ts, D_MODEL) @ w_o_attn[j]
            kp_l.append(kp)
            vp_l.append(vp)
            ks_l.append(kss)
            vs_l.append(vss)
        else:
            qp, kp, vp, gp = _retention_qkvg(hp, w_in_ret[j])
            op, s_fin = _retention_prompt(qp, kp, vp, log_gamma)
            ap = _retention_out(op, gp, w_o_ret[j])
            qs, kss, vss, gs = _retention_qkvg(hs, w_in_ret[j])
            os_, s_new = _retention_chunk(qs, kss, vss, state_ret[j].astype(jnp.float32), log_gamma)
            a_s = _retention_out(os_, gs, w_o_ret[j])
            sp_l.append(s_fin)
            ss_l.append(s_new)
        yp = _layer_norm(DN_ALPHA * yp + mp[2][:, None, :] * ap, ln_g[i, 0], ln_b[i, 0])
        ys = _layer_norm(DN_ALPHA * ys + ms[2][:, None, :] * a_s, ln_g[i, 0], ln_b[i, 0])
        fp = _peer(_modulate(yp, mp[3], mp[4]), w_q_peer[i], keys_peer[i], u_peer[i], v_peer[i])
        fs = _peer(_modulate(ys, ms[3], ms[4]), w_q_peer[i], keys_peer[i], u_peer[i], v_peer[i])
        yp = _layer_norm(DN_ALPHA * yp + mp[5][:, None, :] * fp, ln_g[i, 1], ln_b[i, 1])
        ys = _layer_norm(DN_ALPHA * ys + ms[5][:, None, :] * fs, ln_g[i, 1], ln_b[i, 1])
    y_prompt = yp
    y_sample = ys
    k_prompt = jnp.stack(kp_l)
    v_prompt = jnp.stack(vp_l)
    k_sample = jnp.stack(ks_l)
    v_sample = jnp.stack(vs_l)
    ret_state_prompt = jnp.stack(sp_l)
    ret_state_sample = jnp.stack(ss_l)
    return (y_prompt, y_sample, k_prompt, v_prompt, k_sample, v_sample, ret_state_prompt, ret_state_sample)
```

```python
import functools

import jax
import jax.numpy as jnp
import numpy as np
from jax import lax
from jax.experimental import pallas as pl
from jax.experimental.pallas import tpu as pltpu

f32 = jnp.float32
bf16 = jnp.bfloat16
i32 = jnp.int32

D_MODEL = 1024
DEPTH = 2
PAGE_SIZE = 128
N_HEADS = 16
HEAD_DIM = D_MODEL // N_HEADS
MOBA_BLOCK = 256
MOBA_TOPK = 3
R_HEADS = 4
R_DK = D_MODEL // R_HEADS
R_DV = 2 * R_DK
R_Q = R_HEADS * R_DK
R_V = R_HEADS * R_DV
R_IN = 2 * R_Q + 2 * R_V
R_CHUNK = 128
PEER_HEADS = 8
PEER_NKEYS = 128
PEER_EXPERTS = PEER_NKEYS * PEER_NKEYS
PEER_HALF = 128
PEER_TOPK = 16
DN_ALPHA = (2 * DEPTH) ** 0.25
LN_EPS = 1e-5

V7X_LANES = 128
V7X_SUBLANES = 8
V7X_VMEM_BYTES = 64 * 1024 * 1024

NEG = -1e30
INV_SQRT2 = 0.7071067811865476

NT_DIMS = (((1,), (1,)), ((), ()))
TN_DIMS = (((0,), (0,)), ((), ()))


def _vmem(mib):
    assert mib * 1024 * 1024 < V7X_VMEM_BYTES
    return mib * 1024 * 1024


def _params(sem, mib):
    return pltpu.CompilerParams(dimension_semantics=sem, vmem_limit_bytes=_vmem(mib))


def _modulate(x, shift, scale):
    return x * (1.0 + scale) + shift


def _layer_norm(y, g, b):
    mu = jnp.mean(y, axis=-1, keepdims=True)
    yc = y - mu
    var = jnp.mean(yc * yc, axis=-1, keepdims=True)
    return yc * lax.rsqrt(var + LN_EPS) * g + b


class _Mod:
    def __init__(self, vec, tm, rows_per_group):
        self.tm = tm
        if rows_per_group % tm == 0:
            self.arr = vec[:, None, :]
            self.tiles_per_group = rows_per_group // tm
            self.per_row = False
        else:
            self.arr = jnp.repeat(vec, rows_per_group, axis=0)
            self.per_row = True

    def spec(self, row_of):
        d = self.arr.shape[-1]
        if self.per_row:
            return pl.BlockSpec((self.tm, d), lambda *g: (row_of(*g), 0))
        tpg = self.tiles_per_group
        return pl.BlockSpec((None, 1, d), lambda *g: (row_of(*g) // tpg, 0, 0))


def _ada_body(c_ref, w_ref, b_ref, o_ref):
    c = c_ref[...]
    a = c * (1.0 / (1.0 + jnp.exp(-c)))
    o_ref[...] = jnp.dot(a, w_ref[...], preferred_element_type=f32,
                         precision=lax.Precision.HIGHEST) + b_ref[...]


def _ada(c_all, w_ada, b_ada):
    depth, d, n6 = w_ada.shape
    rows = c_all.shape[0]
    tn = 1024
    return pl.pallas_call(
        _ada_body,
        grid=(depth, n6 // tn),
        in_specs=[pl.BlockSpec((rows, d), lambda l, j: (0, 0)),
                  pl.BlockSpec((None, d, tn), lambda l, j: (l, 0, j)),
                  pl.BlockSpec((None, 1, tn), lambda l, j: (l, 0, j))],
        out_specs=pl.BlockSpec((None, rows, tn), lambda l, j: (l, 0, j)),
        out_shape=jax.ShapeDtypeStruct((depth, rows, n6), f32),
        compiler_params=_params(("arbitrary", "arbitrary"), 24),
        name="ada",
    )(c_all, w_ada, b_ada.reshape(depth, 1, n6))


def _mod_mm_body(x_ref, sh_ref, sc_ref, w_ref, o_ref):
    h = _modulate(x_ref[...], sh_ref[...], sc_ref[...]).astype(bf16)
    n = o_ref.shape[1]
    for c in range(0, n, 512):
        o_ref[:, c:c + 512] = jnp.dot(h, w_ref[:, c:c + 512], preferred_element_type=f32)


def _mod_matmul(x, shift, scale, w, tm, tn, name):
    m, d = x.shape
    n = w.shape[1]
    row_of = lambda j, i: i
    return pl.pallas_call(
        _mod_mm_body,
        grid=(n // tn, m // tm),
        in_specs=[pl.BlockSpec((tm, d), lambda j, i: (i, 0)),
                  shift.spec(row_of), scale.spec(row_of),
                  pl.BlockSpec((d, tn), lambda j, i: (0, j))],
        out_specs=pl.BlockSpec((tm, tn), lambda j, i: (i, j)),
        out_shape=jax.ShapeDtypeStruct((m, n), f32),
        compiler_params=_params(("arbitrary", "arbitrary"), 40),
        name=name,
    )(x, shift.arr, scale.arr, w)


def _qkv_body(x_ref, sh_ref, sc_ref, w_ref, q_ref, qb_ref, k_ref, kb_ref, v_ref, vb_ref):
    h = _modulate(x_ref[...], sh_ref[...], sc_ref[...]).astype(bf16)
    d = D_MODEL
    for c in range(0, d, 512):
        q = jnp.dot(h, w_ref[:, c:c + 512], preferred_element_type=f32)
        q_ref[:, c:c + 512] = q
        qb_ref[:, c:c + 512] = (q * (HEAD_DIM ** -0.5)).astype(bf16)
        k = jnp.dot(h, w_ref[:, d + c:d + c + 512], preferred_element_type=f32)
        k_ref[:, c:c + 512] = k
        kb_ref[:, c:c + 512] = k.astype(bf16)
        v = jnp.dot(h, w_ref[:, 2 * d + c:2 * d + c + 512], preferred_element_type=f32)
        v_ref[:, c:c + 512] = v
        vb_ref[:, c:c + 512] = v.astype(bf16)


def _qkv(x, shift, scale, w, tm, name):
    m, d = x.shape
    row_of = lambda i: i
    blk = pl.BlockSpec((tm, d), lambda i: (i, 0))
    return pl.pallas_call(
        _qkv_body,
        grid=(m // tm,),
        in_specs=[blk, shift.spec(row_of), scale.spec(row_of),
                  pl.BlockSpec((d, 3 * d), lambda i: (0, 0))],
        out_specs=[blk] * 6,
        out_shape=[jax.ShapeDtypeStruct((m, d), t) for t in (f32, bf16, f32, bf16, f32, bf16)],
        compiler_params=_params(("arbitrary",), 48),
        name=name,
    )(x, shift.arr, scale.arr, w)


def _proj_ln_body(a_ref, w_ref, x_ref, gate_ref, g_ref, b_ref, o_ref):
    f = jnp.dot(a_ref[...].astype(bf16), w_ref[...], preferred_element_type=f32)
    y = DN_ALPHA * x_ref[...] + gate_ref[...] * f
    o_ref[...] = _layer_norm(y, g_ref[...], b_ref[...])


def _proj_res_ln(a, w, x, gate, ln_g, ln_b, tm, name):
    m, k = a.shape
    d = x.shape[1]
    row_of = lambda i: i
    vec = pl.BlockSpec((1, d), lambda i: (0, 0))
    return pl.pallas_call(
        _proj_ln_body,
        grid=(m // tm,),
        in_specs=[pl.BlockSpec((tm, k), lambda i: (i, 0)),
                  pl.BlockSpec((k, d), lambda i: (0, 0)),
                  pl.BlockSpec((tm, d), lambda i: (i, 0)),
                  gate.spec(row_of), vec, vec],
        out_specs=pl.BlockSpec((tm, d), lambda i: (i, 0)),
        out_shape=jax.ShapeDtypeStruct((m, d), f32),
        compiler_params=_params(("arbitrary",), 40),
        name=name,
    )(a, w, x, gate.arr, ln_g.reshape(1, d), ln_b.reshape(1, d))


def _kmean_body(k_ref, o_ref):
    o_ref[...] = jnp.sum(k_ref[...], axis=0, keepdims=True) * (1.0 / MOBA_BLOCK)


def _block_means(k):
    m, d = k.shape
    nb = m // MOBA_BLOCK
    return pl.pallas_call(
        _kmean_body,
        grid=(nb,),
        in_specs=[pl.BlockSpec((MOBA_BLOCK, d), lambda i: (i, 0))],
        out_specs=pl.BlockSpec((None, 1, d), lambda i: (i, 0, 0)),
        out_shape=jax.ShapeDtypeStruct((nb, 1, d), f32),
        compiler_params=_params(("arbitrary",), 16),
        name="moba_block_means",
    )(k)


def _moba_body(sl_ref, q_ref, k_ref, vt_ref, km_ref, o_ref, sel_ref):
    h = pl.program_id(1)
    qt = pl.program_id(2)
    slope = sl_ref[h]
    L = MOBA_BLOCK
    nb = km_ref.shape[0]
    q = q_ref[...]

    sc = lax.dot_general(km_ref[...].astype(bf16), q, NT_DIMS, preferred_element_type=f32)
    rows = lax.broadcasted_iota(i32, sc.shape, 0)
    sc = jnp.where(rows < qt, sc, -jnp.inf)
    sel = jnp.zeros(sc.shape, f32)
    for r in range(MOBA_TOPK):
        mx = jnp.max(sc, axis=0, keepdims=True)
        first = jnp.min(jnp.where(sc == mx, rows, nb), axis=0, keepdims=True)
        pick = rows == first
        slot_ok = jnp.where(qt > r, 1.0, 0.0)
        sel = jnp.maximum(sel, jnp.where(pick, slot_ok, 0.0))
        sc = jnp.where(pick, -jnp.inf, sc)
    sel_ref[...] = sel

    ki = lax.broadcasted_iota(i32, (L, L), 0)
    qi = lax.broadcasted_iota(i32, (L, L), 1)
    base = (-slope) * (qi - ki).astype(f32)

    s = lax.dot_general(k_ref[qt], q, NT_DIMS, preferred_element_type=f32) + base
    s = jnp.where(ki <= qi, s, NEG)
    m0 = jnp.max(s, axis=0, keepdims=True)
    p = jnp.exp(s - m0)
    l0 = jnp.sum(p, axis=0, keepdims=True)
    acc0 = jnp.dot(vt_ref[qt], p.astype(bf16), preferred_element_type=f32)

    def step(mb, carry):
        m, l, acc = carry
        cm = (qt - mb).astype(f32) * ((-slope) * L)
        addrow = jnp.where(sel_ref[pl.ds(mb, 1), :] > 0.0, cm, NEG)
        s = lax.dot_general(k_ref[mb], q, NT_DIMS, preferred_element_type=f32) + base + addrow
        m_new = jnp.maximum(m, jnp.max(s, axis=0, keepdims=True))
        alpha = jnp.exp(m - m_new)
        p = jnp.exp(s - m_new)
        l = l * alpha + jnp.sum(p, axis=0, keepdims=True)
        acc = acc * alpha + jnp.dot(vt_ref[mb], p.astype(bf16), preferred_element_type=f32)
        return m_new, l, acc

    _, l, acc = lax.fori_loop(0, qt, step, (m0, l0, acc0))
    o_ref[...] = acc / l


def _moba_prompt(qb, kb, vb, kmean, slopes, b, t):
    nh, dh, L = N_HEADS, HEAD_DIM, MOBA_BLOCK
    nb = t // L
    qh = qb.reshape(b, t, nh, dh).transpose(0, 2, 1, 3)
    kh = kb.reshape(b, nb, L, nh, dh).transpose(0, 3, 1, 2, 4)
    vt = vb.reshape(b, nb, L, nh, dh).transpose(0, 3, 1, 4, 2)
    km = kmean.reshape(b, nb, nh, dh).transpose(0, 2, 1, 3)
    out_t = pl.pallas_call(
        _moba_body,
        grid_spec=pltpu.PrefetchScalarGridSpec(
            num_scalar_prefetch=1,
            grid=(b, nh, nb),
            in_specs=[pl.BlockSpec((None, None, L, dh), lambda bi, h, i, sl: (bi, h, i, 0)),
                      pl.BlockSpec((None, None, nb, L, dh), lambda bi, h, i, sl: (bi, h, 0, 0, 0)),
                      pl.BlockSpec((None, None, nb, dh, L), lambda bi, h, i, sl: (bi, h, 0, 0, 0)),
                      pl.BlockSpec((None, None, nb, dh), lambda bi, h, i, sl: (bi, h, 0, 0))],
            out_specs=pl.BlockSpec((None, None, dh, L), lambda bi, h, i, sl: (bi, h, 0, i)),
            scratch_shapes=[pltpu.VMEM((nb, L), f32)]),
        out_shape=jax.ShapeDtypeStruct((b, nh, dh, t), f32),
        compiler_params=_params(("arbitrary", "arbitrary", "arbitrary"), 32),
        name="moba_prompt_attn",
    )(slopes, qh, kh, vt, km)
    return out_t.transpose(0, 3, 1, 2).reshape(b * t, nh * dh)


SAMPLE_PAGES_PER_STEP = 8


def _sample_scores_body(pt_ref, *refs):
    npg = SAMPLE_PAGES_PER_STEP
    k_refs, q_ref, o_ref = refs[:npg], refs[npg], refs[npg + 1]
    j = pl.program_id(1)
    ppb = MOBA_BLOCK // PAGE_SIZE
    for bb in range(npg // ppb):
        tot = k_refs[bb * ppb][...]
        for pg in range(1, ppb):
            tot = tot + k_refs[bb * ppb + pg][...]
        kmean = jnp.sum(tot, axis=-1, keepdims=True) * (1.0 / MOBA_BLOCK)
        o_ref[j * (npg // ppb) + bb] = jnp.sum(kmean * q_ref[...], axis=1)


def _sample_block_scores(ck_t, layer, page_table, q_t):
    db, n_pages = page_table.shape
    _, _, nh, dh, pg = ck_t.shape
    tn = q_t.shape[-1]
    npg = SAMPLE_PAGES_PER_STEP
    nbp = n_pages * PAGE_SIZE // MOBA_BLOCK

    def page_spec(o):
        return pl.BlockSpec((None, None, nh, dh, pg), lambda s, j, pt: (layer, pt[s, j * npg + o], 0, 0, 0))

    return pl.pallas_call(
        _sample_scores_body,
        grid_spec=pltpu.PrefetchScalarGridSpec(
            num_scalar_prefetch=1,
            grid=(db, n_pages // npg),
            in_specs=[page_spec(o) for o in range(npg)]
            + [pl.BlockSpec((None, nh, dh, tn), lambda s, j, pt: (s, 0, 0, 0))],
            out_specs=pl.BlockSpec((None, nbp, nh, tn), lambda s, j, pt: (s, 0, 0, 0))),
        out_shape=jax.ShapeDtypeStruct((db, nbp, nh, tn), f32),
        compiler_params=_params(("arbitrary", "arbitrary"), 32),
        name="moba_sample_block_scores",
    )(page_table, *([ck_t] * npg), q_t)


def _sample_topk_body(s_ref, o_ref):
    s = s_ref[...]
    nbp = s.shape[0]
    rows = lax.broadcasted_iota(i32, s.shape, 0)
    for r in range(MOBA_TOPK):
        mx = jnp.max(s, axis=0, keepdims=True)
        first = jnp.min(jnp.where(s == mx, rows, nbp), axis=0, keepdims=True)
        o_ref[r:r + 1, :] = jnp.minimum(first, nbp - 1)
        s = jnp.where(rows == first, -jnp.inf, s)


def _sample_topk(scores):
    db, nbp, c = scores.shape
    return pl.pallas_call(
        _sample_topk_body,
        grid=(db,),
        in_specs=[pl.BlockSpec((None, nbp, c), lambda s: (s, 0, 0))],
        out_specs=pl.BlockSpec((None, MOBA_TOPK, c), lambda s: (s, 0, 0)),
        out_shape=jax.ShapeDtypeStruct((db, MOBA_TOPK, c), i32),
        compiler_params=_params(("arbitrary",), 16),
        name="moba_sample_topk",
    )(scores)


def _sample_page_copies(sel_ref, pt_ref, ck_ref, cv_ref, kbuf, vbuf, sem, step, slot, layer, tn):
    ppb = MOBA_BLOCK // PAGE_SIZE
    s = step // N_HEADS
    h = step % N_HEADS
    out = []
    for qi in range(tn):
        for sl in range(MOBA_TOPK):
            blk = sel_ref[(step * tn + qi) * MOBA_TOPK + sl]
            for pg in range(ppb):
                page = pt_ref[s, blk * ppb + pg]
                idx = (qi * MOBA_TOPK + sl) * ppb + pg
                out.append(pltpu.make_async_copy(ck_ref.at[layer, page, h], kbuf.at[slot, idx], sem.at[slot]))
                out.append(pltpu.make_async_copy(cv_ref.at[layer, page, h], vbuf.at[slot, idx], sem.at[slot]))
    return out


def _sample_attn_body(sel_ref, pt_ref, sl_ref, q_ref, kn_ref, vn_ref, ck_ref, cv_ref, o_ref,
                      kbuf, vbuf, sem, *, layer, past_len):
    tn = q_ref.shape[1]
    ppb = MOBA_BLOCK // PAGE_SIZE
    n = pl.program_id(0)
    nsteps = pl.num_programs(0)
    slot = n % 2
    args = (sel_ref, pt_ref, ck_ref, cv_ref, kbuf, vbuf, sem)

    @pl.when(n == 0)
    def _():
        for c in _sample_page_copies(*args, 0, 0, layer, tn):
            c.start()

    @pl.when(n + 1 < nsteps)
    def _():
        for c in _sample_page_copies(*args, n + 1, 1 - slot, layer, tn):
            c.start()

    for c in _sample_page_copies(*args, n, slot, layer, tn):
        c.wait()

    h = n % N_HEADS
    slope = sl_ref[h]
    lane = lax.broadcasted_iota(i32, (1, PAGE_SIZE), 1).astype(f32)
    jn = lax.broadcasted_iota(i32, (1, tn), 1)
    kn = kn_ref[...]
    vn = vn_ref[...]
    for qi in range(tn):
        qcol = q_ref[:, qi:qi + 1] * (HEAD_DIM ** -0.5)
        t_q = past_len + qi
        s_rows = []
        for sl in range(MOBA_TOPK):
            blk = sel_ref[(n * tn + qi) * MOBA_TOPK + sl]
            for pg in range(ppb):
                idx = (qi * MOBA_TOPK + sl) * ppb + pg
                sc = jnp.sum(kbuf[slot, idx] * qcol, axis=0, keepdims=True)
                dist0 = (t_q - (blk * MOBA_BLOCK + pg * PAGE_SIZE)).astype(f32)
                s_rows.append(sc - slope * (dist0 - lane))
        s_own = jnp.sum(kn * qcol, axis=0, keepdims=True)
        s_own = jnp.where(jn <= qi, s_own - slope * (qi - jn).astype(f32), NEG)
        m = jnp.max(s_own, axis=1, keepdims=True)
        for sc in s_rows:
            m = jnp.maximum(m, jnp.max(sc, axis=1, keepdims=True))
        p_own = jnp.exp(s_own - m)
        l = jnp.sum(p_own, axis=1, keepdims=True)
        acc = jnp.zeros((HEAD_DIM, PAGE_SIZE), f32)
        for r, sc in enumerate(s_rows):
            p = jnp.exp(sc - m)
            l = l + jnp.sum(p, axis=1, keepdims=True)
            acc = acc + vbuf[slot, qi * MOBA_TOPK * ppb + r] * p
        o = jnp.sum(acc, axis=1, keepdims=True) + jnp.sum(vn * p_own, axis=1, keepdims=True)
        o_ref[:, qi:qi + 1] = o / l


def _sample_attention(sel_flat, page_table, slopes, q_t, kn_t, vn_t, ck_t, cv_t, layer):
    db, nh, dh, tn = q_t.shape
    past_len = page_table.shape[1] * PAGE_SIZE
    assert past_len % MOBA_BLOCK == 0
    npg = tn * MOBA_TOPK * (MOBA_BLOCK // PAGE_SIZE)
    blk = pl.BlockSpec((None, None, dh, tn), lambda n, *_: (n // N_HEADS, n % N_HEADS, 0, 0))
    return pl.pallas_call(
        functools.partial(_sample_attn_body, layer=layer, past_len=past_len),
        grid_spec=pltpu.PrefetchScalarGridSpec(
            num_scalar_prefetch=3,
            grid=(db * nh,),
            in_specs=[blk, blk, blk,
                      pl.BlockSpec(memory_space=pl.ANY), pl.BlockSpec(memory_space=pl.ANY)],
            out_specs=blk,
            scratch_shapes=[pltpu.VMEM((2, npg, dh, PAGE_SIZE), f32),
                            pltpu.VMEM((2, npg, dh, PAGE_SIZE), f32),
                            pltpu.SemaphoreType.DMA((2,))]),
        out_shape=jax.ShapeDtypeStruct((db, nh, dh, tn), f32),
        compiler_params=_params(("arbitrary",), 16),
        name="moba_sample_attn",
    )(sel_flat, page_table, slopes, q_t, kn_t, vn_t, ck_t, cv_t)


def _retention_tables(c_len, rows):
    lg = jnp.log(1.0 - 2.0 ** (-5.0 - jnp.arange(R_HEADS, dtype=f32)))
    idx = jnp.arange(rows, dtype=f32)
    live = idx < c_len
    diff = idx[:, None] - idx[None, :]
    decay = jnp.where((diff >= 0) & live[:, None] & live[None, :],
                      jnp.exp(lg[:, None, None] * jnp.maximum(diff, 0.0)), 0.0)
    q_dec = jnp.where(live, jnp.exp(lg[:, None] * (idx + 1.0)), 0.0)
    k_dec = jnp.where(live, jnp.exp(lg[:, None] * (c_len - 1.0 - idx)), 0.0)
    g_c = jnp.exp(lg * c_len)
    q_dec = jnp.broadcast_to(q_dec[:, :, None], (R_HEADS, rows, R_DK))
    k_dec = jnp.broadcast_to(k_dec[:, :, None], (R_HEADS, rows, R_DK))
    return decay, q_dec, k_dec, g_c


def _retention_chunk(q, k, v, g, dec, qd, kd, g_c, s_ref):
    k = k * (R_DK ** -0.5)
    vb = v.astype(bf16)
    inner = lax.dot_general(q.astype(bf16), k.astype(bf16), NT_DIMS, preferred_element_type=f32) * dec
    o = jnp.dot(inner.astype(bf16), vb, preferred_element_type=f32)
    o = o + jnp.dot((q * qd).astype(bf16), s_ref[...].astype(bf16), preferred_element_type=f32)
    s_ref[...] = s_ref[...] * g_c + lax.dot_general((k * kd).astype(bf16), vb, TN_DIMS,
                                                    preferred_element_type=f32)
    mu = jnp.mean(o, axis=-1, keepdims=True)
    oc = o - mu
    var = jnp.mean(oc * oc, axis=-1, keepdims=True)
    on = oc * lax.rsqrt(var + LN_EPS)
    return (g * (1.0 / (1.0 + jnp.exp(-g)))) * on


def _ret_prompt_body(gc_ref, q_ref, k_ref, v_ref, g_ref, dec_ref, qd_ref, kd_ref, o_ref, sfin_ref, s_ref):
    h = pl.program_id(1)
    c = pl.program_id(2)

    @pl.when(c == 0)
    def _():
        s_ref[...] = jnp.zeros(s_ref.shape, f32)

    g_c = gc_ref[h]
    C = R_CHUNK
    for cc in range(q_ref.shape[0] // C):
        r = slice(cc * C, (cc + 1) * C)
        o_ref[r, :] = _retention_chunk(q_ref[r, :], k_ref[r, :], v_ref[r, :], g_ref[r, :],
                                       dec_ref[...], qd_ref[...], kd_ref[...], g_c, s_ref)

    @pl.when(c == pl.num_programs(2) - 1)
    def _():
        sfin_ref[...] = s_ref[...]


def _retention_prompt(z, b, t):
    rb = 4 * R_CHUNK
    nrb = t // rb
    decay, q_dec, k_dec, g_c = _retention_tables(R_CHUNK, R_CHUNK)
    row = lambda bi, h, c, gc: bi * nrb + c
    tab = lambda w: pl.BlockSpec((None, R_CHUNK, w), lambda bi, h, c, gc: (h, 0, 0))
    return pl.pallas_call(
        _ret_prompt_body,
        grid_spec=pltpu.PrefetchScalarGridSpec(
            num_scalar_prefetch=1,
            grid=(b, R_HEADS, nrb),
            in_specs=[pl.BlockSpec((rb, R_DK), lambda *g: (row(*g), g[1])),
                      pl.BlockSpec((rb, R_DK), lambda *g: (row(*g), R_Q // R_DK + g[1])),
                      pl.BlockSpec((rb, R_DV), lambda *g: (row(*g), 2 * R_Q // R_DV + g[1])),
                      pl.BlockSpec((rb, R_DV), lambda *g: (row(*g), (2 * R_Q + R_V) // R_DV + g[1])),
                      tab(R_CHUNK), tab(R_DK), tab(R_DK)],
            out_specs=[pl.BlockSpec((rb, R_DV), lambda *g: (row(*g), g[1])),
                       pl.BlockSpec((None, None, R_DK, R_DV), lambda bi, h, c, gc: (bi, h, 0, 0))],
            scratch_shapes=[pltpu.VMEM((R_DK, R_DV), f32)]),
        out_shape=[jax.ShapeDtypeStruct((b * t, R_V), f32),
                   jax.ShapeDtypeStruct((b, R_HEADS, R_DK, R_DV), f32)],
        compiler_params=_params(("arbitrary", "arbitrary", "arbitrary"), 32),
        name="retention_prompt",
    )(g_c, z, z, z, z, decay, q_dec, k_dec)


def _ret_sample_body(gc_ref, q_ref, k_ref, v_ref, g_ref, s0_ref, dec_ref, qd_ref, kd_ref, o_ref, s1_ref):
    h = pl.program_id(1)
    s1_ref[...] = s0_ref[...]
    o_ref[...] = _retention_chunk(q_ref[...], k_ref[...], v_ref[...], g_ref[...],
                                  dec_ref[...], qd_ref[...], kd_ref[...], gc_ref[h], s1_ref)


def _retention_sample(z, state, db, ts):
    rows = V7X_SUBLANES
    assert ts <= rows
    zp = jnp.pad(z.reshape(db, ts, R_IN), ((0, 0), (0, rows - ts), (0, 0)))
    decay, q_dec, k_dec, g_c = _retention_tables(ts, rows)
    tab = lambda w: pl.BlockSpec((None, rows, w), lambda s, h, gc: (h, 0, 0))
    st = pl.BlockSpec((None, None, R_DK, R_DV), lambda s, h, gc: (s, h, 0, 0))
    o, s1 = pl.pallas_call(
        _ret_sample_body,
        grid_spec=pltpu.PrefetchScalarGridSpec(
            num_scalar_prefetch=1,
            grid=(db, R_HEADS),
            in_specs=[pl.BlockSpec((None, rows, R_DK), lambda s, h, gc: (s, 0, h)),
                      pl.BlockSpec((None, rows, R_DK), lambda s, h, gc: (s, 0, R_Q // R_DK + h)),
                      pl.BlockSpec((None, rows, R_DV), lambda s, h, gc: (s, 0, 2 * R_Q // R_DV + h)),
                      pl.BlockSpec((None, rows, R_DV), lambda s, h, gc: (s, 0, (2 * R_Q + R_V) // R_DV + h)),
                      st, tab(rows), tab(R_DK), tab(R_DK)],
            out_specs=[pl.BlockSpec((None, rows, R_DV), lambda s, h, gc: (s, 0, h)), st]),
        out_shape=[jax.ShapeDtypeStruct((db, rows, R_V), f32),
                   jax.ShapeDtypeStruct(state.shape, f32)],
        compiler_params=_params(("arbitrary", "arbitrary"), 16),
        name="retention_sample",
    )(g_c, zp, zp, zp, zp, state, decay, q_dec, k_dec)
    return o[:, :ts].reshape(db * ts, R_V), s1


PEER_CELLS = [(a, b) for a in range(PEER_TOPK) for b in range(PEER_TOPK) if (a + 1) * (b + 1) <= PEER_TOPK]
PEER_NCELL = len(PEER_CELLS)
PEER_NCELL_PAD = -(-PEER_NCELL // V7X_SUBLANES) * V7X_SUBLANES


def _peer_cell_tables():
    p0 = np.zeros((PEER_NCELL_PAD, PEER_TOPK), np.float32)
    p1 = np.zeros((PEER_NCELL_PAD, PEER_TOPK), np.float32)
    for x, (a, b) in enumerate(PEER_CELLS):
        p0[x, a] = 1.0
        p1[x, b] = 1.0
    return jnp.asarray(p0), jnp.asarray(p1), jnp.asarray(p0.T.copy())


def _top16(s):
    nk = s.shape[0]
    rows = lax.broadcasted_iota(i32, s.shape, 0)
    vrow = lax.broadcasted_iota(i32, (PEER_TOPK, s.shape[1]), 0)
    rk = jnp.full(s.shape, float(PEER_TOPK), f32)
    vals = jnp.zeros((PEER_TOPK, s.shape[1]), f32)
    for r in range(PEER_TOPK):
        mx = jnp.max(s, axis=0, keepdims=True)
        first = jnp.min(jnp.where(s == mx, rows, nk), axis=0, keepdims=True)
        pick = rows == first
        rk = jnp.where(pick, float(r), rk)
        s = jnp.where(pick, -jnp.inf, s)
        vals = jnp.where(vrow == r, mx, vals)
    return vals, rk


def _gather_rows(p, v):
    return jnp.dot(p, v, preferred_element_type=f32, precision=lax.Precision.HIGHEST)


def _peer_route_body(x_ref, sh_ref, sc_ref, wq_ref, keys_ref, p0_ref, p1_ref, p0t_ref,
                     r1_ref, e1_ref, c0_ref, ez_ref, q_scr):
    hmod = _modulate(x_ref[...], sh_ref[...], sc_ref[...]).astype(bf16)
    nk = PEER_HALF
    for hp in range(2 * PEER_HEADS):
        q_scr[hp] = jnp.dot(hmod, wq_ref[:, hp * nk:(hp + 1) * nk], preferred_element_type=f32).astype(bf16)

    G = V7X_SUBLANES
    ngrp = PEER_NCELL_PAD // G
    sub = lax.broadcasted_iota(i32, (G, x_ref.shape[0]), 0)

    def head(h, carry):
        s0 = lax.dot_general(keys_ref[2 * h], q_scr[2 * h], NT_DIMS, preferred_element_type=f32)
        s1 = lax.dot_general(keys_ref[2 * h + 1], q_scr[2 * h + 1], NT_DIMS, preferred_element_type=f32)
        v0, rk0 = _top16(s0)
        v1, rk1 = _top16(s1)
        cand = _gather_rows(p0_ref[...], v0) + _gather_rows(p1_ref[...], v1)
        groups = [cand[g * G:(g + 1) * G] for g in range(ngrp)]
        counts = [jnp.zeros(groups[0].shape, f32) for _ in range(ngrp)]
        for y in range(PEER_NCELL):
            gy, ry = divmod(y, G)
            cy = groups[gy][ry:ry + 1]
            for g in range(ngrp):
                gt = jnp.where(cy > groups[g], 1.0, 0.0)
                ge = jnp.where(cy >= groups[g], 1.0, 0.0)
                if g < gy:
                    ahead = gt
                elif g > gy:
                    ahead = ge
                else:
                    ahead = jnp.where(sub > ry, ge, gt)
                counts[g] = counts[g] + ahead
        cell = lax.broadcasted_iota(i32, cand.shape, 0)
        selm = jnp.where(cell < PEER_NCELL,
                         jnp.where(jnp.concatenate(counts, axis=0) < float(PEER_TOPK), 1.0, 0.0), 0.0)
        cnt = jnp.dot(p0t_ref[...], selm, preferred_element_type=f32)
        ev0 = jnp.exp(v0 - v0[0:1])
        ev1 = jnp.exp(v1 - v1[0:1])
        z = jnp.sum(selm * _gather_rows(p0_ref[...], ev0) * _gather_rows(p1_ref[...], ev1),
                    axis=0, keepdims=True)
        c0 = jnp.zeros(s0.shape, f32)
        for a in range(PEER_TOPK):
            c0 = jnp.where(rk0 == float(a), cnt[a:a + 1], c0)
        r1_ref[h] = rk1
        e1_ref[h] = jnp.exp(s1 - v1[0:1])
        c0_ref[h] = c0
        ez_ref[h] = jnp.exp(s0 - v0[0:1]) / z
        return carry

    lax.fori_loop(0, PEER_HEADS, head, 0)


def _peer_route(x, shift, scale, wq, keys, tm, name):
    m, d = x.shape
    p0, p1, p0t = _peer_cell_tables()
    row_of = lambda i: i
    full = lambda a: pl.BlockSpec(a.shape, lambda i: (0,) * a.ndim)
    out = pl.BlockSpec((PEER_HEADS, PEER_NKEYS, tm), lambda i: (0, 0, i))
    return pl.pallas_call(
        _peer_route_body,
        grid=(m // tm,),
        in_specs=[pl.BlockSpec((tm, d), lambda i: (i, 0)), shift.spec(row_of), scale.spec(row_of),
                  full(wq), full(keys), full(p0), full(p1), full(p0t)],
        out_specs=[out] * 4,
        out_shape=[jax.ShapeDtypeStruct((PEER_HEADS, PEER_NKEYS, m), f32)] * 4,
        scratch_shapes=[pltpu.VMEM((2 * PEER_HEADS, tm, PEER_HALF), bf16)],
        compiler_params=_params(("arbitrary",), 40),
        name=name,
    )(x, shift.arr, scale.arr, wq, keys, p0, p1, p0t)


PEER_TE = V7X_SUBLANES * PEER_NKEYS


def _peer_expert_body(x_ref, sh_ref, sc_ref, gate_ref, g_ref, b_ref, u_ref, vt_ref,
                      r1_ref, e1_ref, c0_ref, ez_ref, o_ref, h_scr, acc_scr, hs_scr):
    e = pl.program_id(1)
    tn = x_ref.shape[0]
    nk = PEER_NKEYS

    @pl.when(e == 0)
    def _():
        h_scr[...] = _modulate(x_ref[...], sh_ref[...], sc_ref[...]).astype(bf16)
        acc_scr[...] = jnp.zeros(acc_scr.shape, f32)

    a_t = lax.dot_general(u_ref[...], h_scr[...], NT_DIMS, preferred_element_type=f32)
    igrp = pl.ds(pl.multiple_of(e * V7X_SUBLANES, V7X_SUBLANES), V7X_SUBLANES)
    for il in range(PEER_TE // nk):
        for lg in range(tn // V7X_LANES):
            ln = slice(lg * V7X_LANES, (lg + 1) * V7X_LANES)
            gate = jnp.zeros((nk, V7X_LANES), f32)
            for hh in range(PEER_HEADS):
                crow = c0_ref[hh, igrp, ln][il:il + 1]
                zrow = ez_ref[hh, igrp, ln][il:il + 1]
                gate = gate + jnp.where(r1_ref[hh, :, ln] < crow, e1_ref[hh, :, ln] * zrow, 0.0)
            a = a_t[il * nk:(il + 1) * nk, ln]
            act = 0.5 * a * (1.0 + lax.erf(a * INV_SQRT2))
            hs_scr[il * nk:(il + 1) * nk, ln] = (gate * act).astype(bf16)
    acc_scr[...] += jnp.dot(vt_ref[...], hs_scr[...], preferred_element_type=f32)

    @pl.when(e == pl.num_programs(1) - 1)
    def _():
        f = acc_scr[...].T
        y = DN_ALPHA * x_ref[...] + gate_ref[...] * f
        o_ref[...] = _layer_norm(y, g_ref[...], b_ref[...])


def _peer_experts(x, shift, scale, gate, ln_g, ln_b, u, vt, route, tm, name):
    m, d = x.shape
    ne = u.shape[0] // PEER_TE
    row_of = lambda i, e: i
    vec = pl.BlockSpec((1, d), lambda i, e: (0, 0))
    rt = pl.BlockSpec((PEER_HEADS, PEER_NKEYS, tm), lambda i, e: (0, 0, i))
    return pl.pallas_call(
        _peer_expert_body,
        grid=(m // tm, ne),
        in_specs=[pl.BlockSpec((tm, d), lambda i, e: (i, 0)),
                  shift.spec(row_of), scale.spec(row_of), gate.spec(row_of), vec, vec,
                  pl.BlockSpec((PEER_TE, d), lambda i, e: (e, 0)),
                  pl.BlockSpec((d, PEER_TE), lambda i, e: (0, e)),
                  rt, rt, rt, rt],
        out_specs=pl.BlockSpec((tm, d), lambda i, e: (i, 0)),
        out_shape=jax.ShapeDtypeStruct((m, d), f32),
        scratch_shapes=[pltpu.VMEM((tm, d), bf16), pltpu.VMEM((d, tm), f32), pltpu.VMEM((PEER_TE, tm), bf16)],
        compiler_params=_params(("arbitrary", "arbitrary"), 48),
        name=name,
    )(x, shift.arr, scale.arr, gate.arr, ln_g.reshape(1, d), ln_b.reshape(1, d), u, vt, *route)


def kernel(x_prompt, x_sample, cache_k, cache_v, state_ret, page_table, c_prompt, c_sample, w_ada, b_ada,
           ln_g, ln_b, w_qkv_attn, w_o_attn, w_in_ret, w_o_ret, w_q_peer, keys_peer, u_peer, v_peer):
    bp, tp, d = x_prompt.shape
    bs, ts, _ = x_sample.shape
    n_p, n_s = bp * tp, bs * ts
    tm_p = 512
    depth = w_ada.shape[0]

    c_all = jnp.concatenate([c_prompt, c_sample], axis=0)
    pad = (-c_all.shape[0]) % V7X_SUBLANES
    ada = _ada(jnp.pad(c_all, ((0, pad), (0, 0))), w_ada, b_ada)

    slopes = 2.0 ** (-8.0 * jnp.arange(1, N_HEADS + 1, dtype=f32) / N_HEADS)
    ck_t = cache_k.transpose(0, 1, 3, 4, 2)
    cv_t = cache_v.transpose(0, 1, 3, 4, 2)

    yp = x_prompt.reshape(n_p, d)
    ys = x_sample.reshape(n_s, d)
    kp_l, vp_l, ks_l, vs_l, sp_l, ss_l = [], [], [], [], [], []
    for i in range(depth):
        mods_p = [_Mod(ada[i, :bp, c * d:(c + 1) * d], tm_p, tp) for c in range(6)]
        mods_s = [_Mod(ada[i, bp:bp + bs, c * d:(c + 1) * d], n_s, ts) for c in range(6)]
        j = i // 2
        if i % 2 == 0:
            w = w_qkv_attn[j].astype(bf16)
            wo = w_o_attn[j].astype(bf16)
            _, qb, kp, kb, vp, vb = _qkv(yp, mods_p[0], mods_p[1], w, tm_p, "qkv_prompt")
            kmean = _block_means(kp)
            ap = _moba_prompt(qb, kb, vb, kmean, slopes, bp, tp)
            qs, _, kss, _, vss, _ = _qkv(ys, mods_s[0], mods_s[1], w, n_s, "qkv_sample")
            to_t = lambda a: a.reshape(bs, ts, N_HEADS, HEAD_DIM).transpose(0, 2, 3, 1)
            q_t = to_t(qs)
            scores = _sample_block_scores(ck_t, j, page_table, q_t)
            sel = _sample_topk(scores.reshape(bs, scores.shape[1], N_HEADS * ts))
            sel_flat = sel.reshape(bs, MOBA_TOPK, N_HEADS, ts).transpose(0, 2, 3, 1).reshape(-1)
            a_t = _sample_attention(sel_flat, page_table, slopes, q_t, to_t(kss), to_t(vss), ck_t, cv_t, j)
            a_s = a_t.transpose(0, 3, 1, 2).reshape(n_s, d)
            kp_l.append(kp.reshape(bp, tp, N_HEADS, HEAD_DIM))
            vp_l.append(vp.reshape(bp, tp, N_HEADS, HEAD_DIM))
            ks_l.append(kss.reshape(bs, ts, N_HEADS, HEAD_DIM))
            vs_l.append(vss.reshape(bs, ts, N_HEADS, HEAD_DIM))
        else:
            w = w_in_ret[j].astype(bf16)
            wo = w_o_ret[j].astype(bf16)
            zp = _mod_matmul(yp, mods_p[0], mods_p[1], w, tm_p, 2048, "ret_in_prompt")
            ap, s_fin = _retention_prompt(zp, bp, tp)
            zs = _mod_matmul(ys, mods_s[0], mods_s[1], w, n_s, 2048, "ret_in_sample")
            a_s, s_new = _retention_sample(zs, state_ret[j], bs, ts)
            sp_l.append(s_fin)
            ss_l.append(s_new)
        yp = _proj_res_ln(ap, wo, yp, mods_p[2], ln_g[i, 0], ln_b[i, 0], tm_p, "mixer_out_prompt")
        ys = _proj_res_ln(a_s, wo, ys, mods_s[2], ln_g[i, 0], ln_b[i, 0], n_s, "mixer_out_sample")

        wq = w_q_peer[i].astype(bf16)
        keys = keys_peer[i].reshape(2 * PEER_HEADS, PEER_NKEYS, PEER_HALF).astype(bf16)
        u = u_peer[i].astype(bf16)
        vt = v_peer[i].astype(bf16).T
        tr = V7X_LANES
        route_p = _peer_route(yp, _Mod(mods_p[3].arr[:, 0], tr, tp), _Mod(mods_p[4].arr[:, 0], tr, tp),
                              wq, keys, tr, "peer_route_prompt")
        yp = _peer_experts(yp, mods_p[3], mods_p[4], mods_p[5], ln_g[i, 1], ln_b[i, 1], u, vt, route_p,
                           tm_p, "peer_experts_prompt")
        route_s = _peer_route(ys, mods_s[3], mods_s[4], wq, keys, n_s, "peer_route_sample")
        ys = _peer_experts(ys, mods_s[3], mods_s[4], mods_s[5], ln_g[i, 1], ln_b[i, 1], u, vt, route_s,
                           n_s, "peer_experts_sample")

    return (yp.reshape(bp, tp, d), ys.reshape(bs, ts, d),
            jnp.stack(kp_l), jnp.stack(vp_l), jnp.stack(ks_l), jnp.stack(vs_l),
            jnp.stack(sp_l), jnp.stack(ss_l))
```

```python
import functools

import jax
import jax.numpy as jnp
import numpy as np
from jax import lax
from jax.experimental import pallas as pl
from jax.experimental.pallas import tpu as pltpu

f32 = jnp.float32
bf16 = jnp.bfloat16
i32 = jnp.int32

D_MODEL = 1024
DEPTH = 2
PAGE_SIZE = 128
N_HEADS = 16
HEAD_DIM = D_MODEL // N_HEADS
MOBA_BLOCK = 256
MOBA_TOPK = 3
R_HEADS = 4
R_DK = D_MODEL // R_HEADS
R_DV = 2 * R_DK
R_Q = R_HEADS * R_DK
R_V = R_HEADS * R_DV
R_IN = 2 * R_Q + 2 * R_V
R_CHUNK = 128
PEER_HEADS = 8
PEER_NKEYS = 128
PEER_EXPERTS = PEER_NKEYS * PEER_NKEYS
PEER_HALF = 128
PEER_TOPK = 16
DN_ALPHA = (2 * DEPTH) ** 0.25
LN_EPS = 1e-5

V7X_LANES = 128
V7X_SUBLANES = 8
V7X_VMEM_BYTES = 64 * 1024 * 1024

NEG = -1e30
INV_SQRT2 = 0.7071067811865476

NT_DIMS = (((1,), (1,)), ((), ()))
TN_DIMS = (((0,), (0,)), ((), ()))


def _vmem(mib):
    assert mib * 1024 * 1024 < V7X_VMEM_BYTES
    return mib * 1024 * 1024


def _params(sem, mib):
    return pltpu.CompilerParams(dimension_semantics=sem, vmem_limit_bytes=_vmem(mib))


def _modulate(x, shift, scale):
    return x * (1.0 + scale) + shift


def _layer_norm(y, g, b):
    mu = jnp.mean(y, axis=-1, keepdims=True)
    yc = y - mu
    var = jnp.mean(yc * yc, axis=-1, keepdims=True)
    return yc * lax.rsqrt(var + LN_EPS) * g + b


class _Mod:
    def __init__(self, vec, tm, rows_per_group):
        self.tm = tm
        if rows_per_group % tm == 0:
            self.arr = vec[:, None, :]
            self.tiles_per_group = rows_per_group // tm
            self.per_row = False
        else:
            self.arr = jnp.repeat(vec, rows_per_group, axis=0)
            self.per_row = True

    def spec(self, row_of):
        d = self.arr.shape[-1]
        if self.per_row:
            return pl.BlockSpec((self.tm, d), lambda *g: (row_of(*g), 0))
        tpg = self.tiles_per_group
        return pl.BlockSpec((None, 1, d), lambda *g: (row_of(*g) // tpg, 0, 0))


def _ada_body(c_ref, w_ref, b_ref, o_ref):
    c = c_ref[...]
    a = c * (1.0 / (1.0 + jnp.exp(-c)))
    o_ref[...] = jnp.dot(a, w_ref[...], preferred_element_type=f32,
                         precision=lax.Precision.HIGHEST) + b_ref[...]


def _ada(c_all, w_ada, b_ada):
    depth, d, n6 = w_ada.shape
    rows = c_all.shape[0]
    tn = 1024
    return pl.pallas_call(
        _ada_body,
        grid=(depth, n6 // tn),
        in_specs=[pl.BlockSpec((rows, d), lambda l, j: (0, 0)),
                  pl.BlockSpec((None, d, tn), lambda l, j: (l, 0, j)),
                  pl.BlockSpec((None, 1, tn), lambda l, j: (l, 0, j))],
        out_specs=pl.BlockSpec((None, rows, tn), lambda l, j: (l, 0, j)),
        out_shape=jax.ShapeDtypeStruct((depth, rows, n6), f32),
        compiler_params=_params(("arbitrary", "arbitrary"), 24),
        name="ada",
    )(c_all, w_ada, b_ada.reshape(depth, 1, n6))


def _mod_mm_body(x_ref, sh_ref, sc_ref, w_ref, o_ref):
    h = _modulate(x_ref[...], sh_ref[...], sc_ref[...]).astype(bf16)
    n = o_ref.shape[1]
    for c in range(0, n, 512):
        o_ref[:, c:c + 512] = jnp.dot(h, w_ref[:, c:c + 512], preferred_element_type=f32)


def _mod_matmul(x, shift, scale, w, tm, tn, name):
    m, d = x.shape
    n = w.shape[1]
    row_of = lambda j, i: i
    return pl.pallas_call(
        _mod_mm_body,
        grid=(n // tn, m // tm),
        in_specs=[pl.BlockSpec((tm, d), lambda j, i: (i, 0)),
                  shift.spec(row_of), scale.spec(row_of),
                  pl.BlockSpec((d, tn), lambda j, i: (0, j))],
        out_specs=pl.BlockSpec((tm, tn), lambda j, i: (i, j)),
        out_shape=jax.ShapeDtypeStruct((m, n), f32),
        compiler_params=_params(("arbitrary", "arbitrary"), 40),
        name=name,
    )(x, shift.arr, scale.arr, w)


def _qkv_body(x_ref, sh_ref, sc_ref, w_ref, q_ref, qb_ref, k_ref, kb_ref, v_ref, vb_ref):
    h = _modulate(x_ref[...], sh_ref[...], sc_ref[...]).astype(bf16)
    d = D_MODEL
    for c in range(0, d, 512):
        q = jnp.dot(h, w_ref[:, c:c + 512], preferred_element_type=f32)
        q_ref[:, c:c + 512] = q
        qb_ref[:, c:c + 512] = (q * (HEAD_DIM ** -0.5)).astype(bf16)
        k = jnp.dot(h, w_ref[:, d + c:d + c + 512], preferred_element_type=f32)
        k_ref[:, c:c + 512] = k
        kb_ref[:, c:c + 512] = k.astype(bf16)
        v = jnp.dot(h, w_ref[:, 2 * d + c:2 * d + c + 512], preferred_element_type=f32)
        v_ref[:, c:c + 512] = v
        vb_ref[:, c:c + 512] = v.astype(bf16)


def _qkv(x, shift, scale, w, tm, name):
    m, d = x.shape
    row_of = lambda i: i
    blk = pl.BlockSpec((tm, d), lambda i: (i, 0))
    return pl.pallas_call(
        _qkv_body,
        grid=(m // tm,),
        in_specs=[blk, shift.spec(row_of), scale.spec(row_of),
                  pl.BlockSpec((d, 3 * d), lambda i: (0, 0))],
        out_specs=[blk] * 6,
        out_shape=[jax.ShapeDtypeStruct((m, d), t) for t in (f32, bf16, f32, bf16, f32, bf16)],
        compiler_params=_params(("arbitrary",), 48),
        name=name,
    )(x, shift.arr, scale.arr, w)


def _qkv_heads_body(x_ref, sh_ref, sc_ref, w_ref, qh_ref, kh_ref, vt_ref, kt_ref, vtf_ref, km_ref):
    h = _modulate(x_ref[...], sh_ref[...], sc_ref[...]).astype(bf16)
    d, dh, L = D_MODEL, HEAD_DIM, MOBA_BLOCK
    tm = x_ref.shape[0]
    cw = 512
    for c in range(0, d, cw):
        q = jnp.dot(h, w_ref[:, c:c + cw], preferred_element_type=f32) * (dh ** -0.5)
        k = jnp.dot(h, w_ref[:, d + c:d + c + cw], preferred_element_type=f32)
        v = jnp.dot(h, w_ref[:, 2 * d + c:2 * d + c + cw], preferred_element_type=f32)
        k_t = k.T
        v_t = v.T
        for blk in range(tm // L):
            km_ref[blk, :, c:c + cw] = jnp.sum(k[blk * L:(blk + 1) * L], axis=0, keepdims=True) * (1.0 / L)
        for hh in range(cw // dh):
            head = c // dh + hh
            cols = slice(hh * dh, (hh + 1) * dh)
            qh_ref[head] = q[:, cols].astype(bf16)
            kt_ref[head] = k_t[cols, :]
            vtf_ref[head] = v_t[cols, :]
            for blk in range(tm // L):
                kh_ref[head, blk] = k[blk * L:(blk + 1) * L, cols].astype(bf16)
                vt_ref[head, blk] = v_t[cols, blk * L:(blk + 1) * L].astype(bf16)


def _qkv_heads(x, shift, scale, w, b, t, tm, name):
    m, d = x.shape
    nh, dh, L = N_HEADS, HEAD_DIM, MOBA_BLOCK
    nb, tpb, bpt = t // L, t // tm, tm // L
    row_of = lambda bi, i: bi * tpb + i
    return pl.pallas_call(
        _qkv_heads_body,
        grid=(b, tpb),
        in_specs=[pl.BlockSpec((tm, d), lambda bi, i: (bi * tpb + i, 0)),
                  shift.spec(row_of), scale.spec(row_of),
                  pl.BlockSpec((d, 3 * d), lambda bi, i: (0, 0))],
        out_specs=[pl.BlockSpec((None, nh, tm, dh), lambda bi, i: (bi, 0, i, 0)),
                   pl.BlockSpec((None, nh, bpt, L, dh), lambda bi, i: (bi, 0, i, 0, 0)),
                   pl.BlockSpec((None, nh, bpt, dh, L), lambda bi, i: (bi, 0, i, 0, 0)),
                   pl.BlockSpec((None, nh, dh, tm), lambda bi, i: (bi, 0, 0, i)),
                   pl.BlockSpec((None, nh, dh, tm), lambda bi, i: (bi, 0, 0, i)),
                   pl.BlockSpec((None, bpt, 1, d), lambda bi, i: (bi, i, 0, 0))],
        out_shape=[jax.ShapeDtypeStruct((b, nh, t, dh), bf16),
                   jax.ShapeDtypeStruct((b, nh, nb, L, dh), bf16),
                   jax.ShapeDtypeStruct((b, nh, nb, dh, L), bf16),
                   jax.ShapeDtypeStruct((b, nh, dh, t), f32),
                   jax.ShapeDtypeStruct((b, nh, dh, t), f32),
                   jax.ShapeDtypeStruct((b, nb, 1, d), f32)],
        compiler_params=_params(("arbitrary", "arbitrary"), 56),
        name=name,
    )(x, shift.arr, scale.arr, w)


def _proj_ln_body(a_ref, w_ref, x_ref, gate_ref, g_ref, b_ref, o_ref):
    f = jnp.dot(a_ref[...].astype(bf16), w_ref[...], preferred_element_type=f32)
    y = DN_ALPHA * x_ref[...] + gate_ref[...] * f
    o_ref[...] = _layer_norm(y, g_ref[...], b_ref[...])


def _proj_res_ln(a, w, x, gate, ln_g, ln_b, tm, name):
    m, k = a.shape
    d = x.shape[1]
    row_of = lambda i: i
    vec = pl.BlockSpec((1, d), lambda i: (0, 0))
    return pl.pallas_call(
        _proj_ln_body,
        grid=(m // tm,),
        in_specs=[pl.BlockSpec((tm, k), lambda i: (i, 0)),
                  pl.BlockSpec((k, d), lambda i: (0, 0)),
                  pl.BlockSpec((tm, d), lambda i: (i, 0)),
                  gate.spec(row_of), vec, vec],
        out_specs=pl.BlockSpec((tm, d), lambda i: (i, 0)),
        out_shape=jax.ShapeDtypeStruct((m, d), f32),
        compiler_params=_params(("arbitrary",), 40),
        name=name,
    )(a, w, x, gate.arr, ln_g.reshape(1, d), ln_b.reshape(1, d))


def _kmean_body(k_ref, o_ref):
    o_ref[...] = jnp.sum(k_ref[...], axis=0, keepdims=True) * (1.0 / MOBA_BLOCK)


def _block_means(k):
    m, d = k.shape
    nb = m // MOBA_BLOCK
    return pl.pallas_call(
        _kmean_body,
        grid=(nb,),
        in_specs=[pl.BlockSpec((MOBA_BLOCK, d), lambda i: (i, 0))],
        out_specs=pl.BlockSpec((None, 1, d), lambda i: (i, 0, 0)),
        out_shape=jax.ShapeDtypeStruct((nb, 1, d), f32),
        compiler_params=_params(("arbitrary",), 16),
        name="moba_block_means",
    )(k)


MOBA_HEADS_PER_STEP = 8


def _moba_body(sl_ref, q_ref, k_ref, vt_ref, km_ref, o_ref, sel_ref, base_ref):
    hg = q_ref.shape[0]
    dh = q_ref.shape[2]
    h0 = pl.program_id(1) * hg
    qt = pl.program_id(2)
    L = MOBA_BLOCK
    nb = km_ref.shape[0]
    ki = lax.broadcasted_iota(i32, (L, L), 0)
    qi = lax.broadcasted_iota(i32, (L, L), 1)

    @pl.when(qt == 0)
    def _():
        rel = (qi - ki).astype(f32)
        for hh in range(hg):
            base_ref[hh] = (-sl_ref[h0 + hh]) * rel

    state = []
    for hh in range(hg):
        q = q_ref[hh]
        km = km_ref[:, 0, hh * dh:(hh + 1) * dh]
        sc = lax.dot_general(km.astype(bf16), q, NT_DIMS, preferred_element_type=f32)
        rows = lax.broadcasted_iota(i32, sc.shape, 0)
        sc = jnp.where(rows < qt, sc, -jnp.inf)
        sel = jnp.zeros(sc.shape, f32)
        for r in range(MOBA_TOPK):
            mx = jnp.max(sc, axis=0, keepdims=True)
            first = jnp.min(jnp.where(sc == mx, rows, nb), axis=0, keepdims=True)
            pick = rows == first
            slot_ok = jnp.where(qt > r, 1.0, 0.0)
            sel = jnp.maximum(sel, jnp.where(pick, slot_ok, 0.0))
            sc = jnp.where(pick, -jnp.inf, sc)
        sel_ref[hh] = sel
        s = lax.dot_general(k_ref[hh, qt], q, NT_DIMS, preferred_element_type=f32) + base_ref[hh]
        s = jnp.where(ki <= qi, s, NEG)
        m0 = jnp.max(s, axis=0, keepdims=True)
        p = jnp.exp(s - m0)
        l0 = jnp.sum(p, axis=0, keepdims=True)
        state += [m0, l0, jnp.dot(vt_ref[hh, qt], p.astype(bf16), preferred_element_type=f32)]

    def step(mb, carry):
        scores = [lax.dot_general(k_ref[hh, mb], q_ref[hh], NT_DIMS, preferred_element_type=f32)
                  for hh in range(hg)]
        out, pv = [], []
        for hh in range(hg):
            m, l, acc = carry[3 * hh:3 * hh + 3]
            cm = (qt - mb).astype(f32) * ((-sl_ref[h0 + hh]) * L)
            addrow = jnp.where(sel_ref[hh, pl.ds(mb, 1), :] > 0.0, cm, NEG)
            s = scores[hh] + base_ref[hh] + addrow
            m_new = jnp.maximum(m, jnp.max(s, axis=0, keepdims=True))
            alpha = jnp.exp(m - m_new)
            p = jnp.exp(s - m_new)
            l = l * alpha + jnp.sum(p, axis=0, keepdims=True)
            out += [m_new, l, acc * alpha]
            pv.append(p.astype(bf16))
        for hh in range(hg):
            out[3 * hh + 2] = out[3 * hh + 2] + jnp.dot(vt_ref[hh, mb], pv[hh], preferred_element_type=f32)
        return tuple(out)

    final = lax.fori_loop(0, qt, step, tuple(state))
    out_t = jnp.concatenate([final[3 * hh + 2] / final[3 * hh + 1] for hh in range(hg)], axis=0)
    o_ref[...] = out_t.T


def _moba_prompt(qh, kh, vt, km, slopes):
    b, nh, t, dh = qh.shape
    L, hg = MOBA_BLOCK, MOBA_HEADS_PER_STEP
    nb = t // L
    return pl.pallas_call(
        _moba_body,
        grid_spec=pltpu.PrefetchScalarGridSpec(
            num_scalar_prefetch=1,
            grid=(b, nh // hg, nb),
            in_specs=[pl.BlockSpec((None, hg, L, dh), lambda bi, g, i, sl: (bi, g, i, 0)),
                      pl.BlockSpec((None, hg, nb, L, dh), lambda bi, g, i, sl: (bi, g, 0, 0, 0)),
                      pl.BlockSpec((None, hg, nb, dh, L), lambda bi, g, i, sl: (bi, g, 0, 0, 0)),
                      pl.BlockSpec((None, nb, 1, hg * dh), lambda bi, g, i, sl: (bi, 0, 0, g))],
            out_specs=pl.BlockSpec((L, hg * dh), lambda bi, g, i, sl: (bi * nb + i, g)),
            scratch_shapes=[pltpu.VMEM((hg, nb, L), f32), pltpu.VMEM((hg, L, L), f32)]),
        out_shape=jax.ShapeDtypeStruct((b * t, nh * dh), f32),
        compiler_params=_params(("arbitrary", "arbitrary", "arbitrary"), 40),
        name="moba_prompt_attn",
    )(slopes, qh, kh, vt, km)


SAMPLE_PAGES_PER_STEP = 8


def _sample_scores_body(pt_ref, *refs):
    npg = SAMPLE_PAGES_PER_STEP
    k_refs, q_ref, o_ref = refs[:npg], refs[npg], refs[npg + 1]
    j = pl.program_id(1)
    ppb = MOBA_BLOCK // PAGE_SIZE
    for bb in range(npg // ppb):
        tot = k_refs[bb * ppb][...]
        for pg in range(1, ppb):
            tot = tot + k_refs[bb * ppb + pg][...]
        kmean = jnp.sum(tot, axis=-1, keepdims=True) * (1.0 / MOBA_BLOCK)
        o_ref[j * (npg // ppb) + bb] = jnp.sum(kmean * q_ref[...], axis=1)


def _sample_block_scores(ck_t, layer, page_table, q_t):
    db, n_pages = page_table.shape
    _, _, nh, dh, pg = ck_t.shape
    tn = q_t.shape[-1]
    npg = SAMPLE_PAGES_PER_STEP
    nbp = n_pages * PAGE_SIZE // MOBA_BLOCK

    def page_spec(o):
        return pl.BlockSpec((None, None, nh, dh, pg), lambda s, j, pt: (layer, pt[s, j * npg + o], 0, 0, 0))

    return pl.pallas_call(
        _sample_scores_body,
        grid_spec=pltpu.PrefetchScalarGridSpec(
            num_scalar_prefetch=1,
            grid=(db, n_pages // npg),
            in_specs=[page_spec(o) for o in range(npg)]
            + [pl.BlockSpec((None, nh, dh, tn), lambda s, j, pt: (s, 0, 0, 0))],
            out_specs=pl.BlockSpec((None, nbp, nh, tn), lambda s, j, pt: (s, 0, 0, 0))),
        out_shape=jax.ShapeDtypeStruct((db, nbp, nh, tn), f32),
        compiler_params=_params(("arbitrary", "arbitrary"), 32),
        name="moba_sample_block_scores",
    )(page_table, *([ck_t] * npg), q_t)


def _sample_topk_body(s_ref, o_ref):
    s = s_ref[...]
    nbp = s.shape[0]
    rows = lax.broadcasted_iota(i32, s.shape, 0)
    for r in range(MOBA_TOPK):
        mx = jnp.max(s, axis=0, keepdims=True)
        first = jnp.min(jnp.where(s == mx, rows, nbp), axis=0, keepdims=True)
        o_ref[r:r + 1, :] = jnp.minimum(first, nbp - 1)
        s = jnp.where(rows == first, -jnp.inf, s)


def _sample_topk(scores):
    db, nbp, c = scores.shape
    return pl.pallas_call(
        _sample_topk_body,
        grid=(db,),
        in_specs=[pl.BlockSpec((None, nbp, c), lambda s: (s, 0, 0))],
        out_specs=pl.BlockSpec((None, MOBA_TOPK, c), lambda s: (s, 0, 0)),
        out_shape=jax.ShapeDtypeStruct((db, MOBA_TOPK, c), i32),
        compiler_params=_params(("arbitrary",), 16),
        name="moba_sample_topk",
    )(scores)


def _sample_page_copies(sel_ref, pt_ref, ck_ref, cv_ref, kbuf, vbuf, sem, step, slot, layer, tn):
    ppb = MOBA_BLOCK // PAGE_SIZE
    s = step // N_HEADS
    h = step % N_HEADS
    out = []
    for qi in range(tn):
        for sl in range(MOBA_TOPK):
            blk = sel_ref[(step * tn + qi) * MOBA_TOPK + sl]
            for pg in range(ppb):
                page = pt_ref[s, blk * ppb + pg]
                idx = (qi * MOBA_TOPK + sl) * ppb + pg
                out.append(pltpu.make_async_copy(ck_ref.at[layer, page, h], kbuf.at[slot, idx], sem.at[slot]))
                out.append(pltpu.make_async_copy(cv_ref.at[layer, page, h], vbuf.at[slot, idx], sem.at[slot]))
    return out


def _sample_attn_body(sel_ref, pt_ref, sl_ref, q_ref, kn_ref, vn_ref, ck_ref, cv_ref, o_ref,
                      kbuf, vbuf, sem, *, layer, past_len):
    tn = q_ref.shape[1]
    ppb = MOBA_BLOCK // PAGE_SIZE
    n = pl.program_id(0)
    nsteps = pl.num_programs(0)
    slot = n % 2
    args = (sel_ref, pt_ref, ck_ref, cv_ref, kbuf, vbuf, sem)

    @pl.when(n == 0)
    def _():
        for c in _sample_page_copies(*args, 0, 0, layer, tn):
            c.start()

    @pl.when(n + 1 < nsteps)
    def _():
        for c in _sample_page_copies(*args, n + 1, 1 - slot, layer, tn):
            c.start()

    for c in _sample_page_copies(*args, n, slot, layer, tn):
        c.wait()

    h = n % N_HEADS
    slope = sl_ref[h]
    lane = lax.broadcasted_iota(i32, (1, PAGE_SIZE), 1).astype(f32)
    jn = lax.broadcasted_iota(i32, (1, tn), 1)
    kn = kn_ref[...]
    vn = vn_ref[...]
    for qi in range(tn):
        qcol = q_ref[:, qi:qi + 1] * (HEAD_DIM ** -0.5)
        t_q = past_len + qi
        s_rows = []
        for sl in range(MOBA_TOPK):
            blk = sel_ref[(n * tn + qi) * MOBA_TOPK + sl]
            for pg in range(ppb):
                idx = (qi * MOBA_TOPK + sl) * ppb + pg
                sc = jnp.sum(kbuf[slot, idx] * qcol, axis=0, keepdims=True)
                dist0 = (t_q - (blk * MOBA_BLOCK + pg * PAGE_SIZE)).astype(f32)
                s_rows.append(sc - slope * (dist0 - lane))
        s_own = jnp.sum(kn * qcol, axis=0, keepdims=True)
        s_own = jnp.where(jn <= qi, s_own - slope * (qi - jn).astype(f32), NEG)
        m = jnp.max(s_own, axis=1, keepdims=True)
        for sc in s_rows:
            m = jnp.maximum(m, jnp.max(sc, axis=1, keepdims=True))
        p_own = jnp.exp(s_own - m)
        l = jnp.sum(p_own, axis=1, keepdims=True)
        acc = jnp.zeros((HEAD_DIM, PAGE_SIZE), f32)
        for r, sc in enumerate(s_rows):
            p = jnp.exp(sc - m)
            l = l + jnp.sum(p, axis=1, keepdims=True)
            acc = acc + vbuf[slot, qi * MOBA_TOPK * ppb + r] * p
        o = jnp.sum(acc, axis=1, keepdims=True) + jnp.sum(vn * p_own, axis=1, keepdims=True)
        o_ref[:, qi:qi + 1] = o / l


def _sample_attention(sel_flat, page_table, slopes, q_t, kn_t, vn_t, ck_t, cv_t, layer):
    db, nh, dh, tn = q_t.shape
    past_len = page_table.shape[1] * PAGE_SIZE
    assert past_len % MOBA_BLOCK == 0
    npg = tn * MOBA_TOPK * (MOBA_BLOCK // PAGE_SIZE)
    blk = pl.BlockSpec((None, None, dh, tn), lambda n, *_: (n // N_HEADS, n % N_HEADS, 0, 0))
    return pl.pallas_call(
        functools.partial(_sample_attn_body, layer=layer, past_len=past_len),
        grid_spec=pltpu.PrefetchScalarGridSpec(
            num_scalar_prefetch=3,
            grid=(db * nh,),
            in_specs=[blk, blk, blk,
                      pl.BlockSpec(memory_space=pl.ANY), pl.BlockSpec(memory_space=pl.ANY)],
            out_specs=blk,
            scratch_shapes=[pltpu.VMEM((2, npg, dh, PAGE_SIZE), f32),
                            pltpu.VMEM((2, npg, dh, PAGE_SIZE), f32),
                            pltpu.SemaphoreType.DMA((2,))]),
        out_shape=jax.ShapeDtypeStruct((db, nh, dh, tn), f32),
        compiler_params=_params(("arbitrary",), 16),
        name="moba_sample_attn",
    )(sel_flat, page_table, slopes, q_t, kn_t, vn_t, ck_t, cv_t)


def _retention_tables(c_len, rows):
    lg = jnp.log(1.0 - 2.0 ** (-5.0 - jnp.arange(R_HEADS, dtype=f32)))
    idx = jnp.arange(rows, dtype=f32)
    live = idx < c_len
    diff = idx[:, None] - idx[None, :]
    decay = jnp.where((diff >= 0) & live[:, None] & live[None, :],
                      jnp.exp(lg[:, None, None] * jnp.maximum(diff, 0.0)), 0.0)
    q_dec = jnp.where(live, jnp.exp(lg[:, None] * (idx + 1.0)), 0.0)
    k_dec = jnp.where(live, jnp.exp(lg[:, None] * (c_len - 1.0 - idx)), 0.0)
    g_c = jnp.exp(lg * c_len)
    q_dec = jnp.broadcast_to(q_dec[:, :, None], (R_HEADS, rows, R_DK))
    k_dec = jnp.broadcast_to(k_dec[:, :, None], (R_HEADS, rows, R_DK))
    return decay, q_dec, k_dec, g_c


def _retention_chunk(q, k, v, g, dec, qd, kd, g_c, s_ref):
    k = k * (R_DK ** -0.5)
    vb = v.astype(bf16)
    inner = lax.dot_general(q.astype(bf16), k.astype(bf16), NT_DIMS, preferred_element_type=f32) * dec
    o = jnp.dot(inner.astype(bf16), vb, preferred_element_type=f32)
    o = o + jnp.dot((q * qd).astype(bf16), s_ref[...].astype(bf16), preferred_element_type=f32)
    s_ref[...] = s_ref[...] * g_c + lax.dot_general((k * kd).astype(bf16), vb, TN_DIMS,
                                                    preferred_element_type=f32)
    mu = jnp.mean(o, axis=-1, keepdims=True)
    oc = o - mu
    var = jnp.mean(oc * oc, axis=-1, keepdims=True)
    on = oc * lax.rsqrt(var + LN_EPS)
    return (g * (1.0 / (1.0 + jnp.exp(-g)))) * on


def _ret_prompt_body(gc_ref, q_ref, k_ref, v_ref, g_ref, dec_ref, qd_ref, kd_ref, o_ref, sfin_ref, s_ref):
    h = pl.program_id(1)
    c = pl.program_id(2)

    @pl.when(c == 0)
    def _():
        s_ref[...] = jnp.zeros(s_ref.shape, f32)

    g_c = gc_ref[h]
    C = R_CHUNK
    for cc in range(q_ref.shape[0] // C):
        r = slice(cc * C, (cc + 1) * C)
        o_ref[r, :] = _retention_chunk(q_ref[r, :], k_ref[r, :], v_ref[r, :], g_ref[r, :],
                                       dec_ref[...], qd_ref[...], kd_ref[...], g_c, s_ref)

    @pl.when(c == pl.num_programs(2) - 1)
    def _():
        sfin_ref[...] = s_ref[...]


def _retention_prompt(z, b, t):
    rb = 4 * R_CHUNK
    nrb = t // rb
    decay, q_dec, k_dec, g_c = _retention_tables(R_CHUNK, R_CHUNK)
    row = lambda bi, h, c, gc: bi * nrb + c
    tab = lambda w: pl.BlockSpec((None, R_CHUNK, w), lambda bi, h, c, gc: (h, 0, 0))
    return pl.pallas_call(
        _ret_prompt_body,
        grid_spec=pltpu.PrefetchScalarGridSpec(
            num_scalar_prefetch=1,
            grid=(b, R_HEADS, nrb),
            in_specs=[pl.BlockSpec((rb, R_DK), lambda *g: (row(*g), g[1])),
                      pl.BlockSpec((rb, R_DK), lambda *g: (row(*g), R_Q // R_DK + g[1])),
                      pl.BlockSpec((rb, R_DV), lambda *g: (row(*g), 2 * R_Q // R_DV + g[1])),
                      pl.BlockSpec((rb, R_DV), lambda *g: (row(*g), (2 * R_Q + R_V) // R_DV + g[1])),
                      tab(R_CHUNK), tab(R_DK), tab(R_DK)],
            out_specs=[pl.BlockSpec((rb, R_DV), lambda *g: (row(*g), g[1])),
                       pl.BlockSpec((None, None, R_DK, R_DV), lambda bi, h, c, gc: (bi, h, 0, 0))],
            scratch_shapes=[pltpu.VMEM((R_DK, R_DV), f32)]),
        out_shape=[jax.ShapeDtypeStruct((b * t, R_V), f32),
                   jax.ShapeDtypeStruct((b, R_HEADS, R_DK, R_DV), f32)],
        compiler_params=_params(("arbitrary", "arbitrary", "arbitrary"), 32),
        name="retention_prompt",
    )(g_c, z, z, z, z, decay, q_dec, k_dec)


def _ret_sample_body(gc_ref, q_ref, k_ref, v_ref, g_ref, s0_ref, dec_ref, qd_ref, kd_ref, o_ref, s1_ref):
    h = pl.program_id(1)
    s1_ref[...] = s0_ref[...]
    o_ref[...] = _retention_chunk(q_ref[...], k_ref[...], v_ref[...], g_ref[...],
                                  dec_ref[...], qd_ref[...], kd_ref[...], gc_ref[h], s1_ref)


def _retention_sample(z, state, db, ts):
    rows = V7X_SUBLANES
    assert ts <= rows
    zp = jnp.pad(z.reshape(db, ts, R_IN), ((0, 0), (0, rows - ts), (0, 0)))
    decay, q_dec, k_dec, g_c = _retention_tables(ts, rows)
    tab = lambda w: pl.BlockSpec((None, rows, w), lambda s, h, gc: (h, 0, 0))
    st = pl.BlockSpec((None, None, R_DK, R_DV), lambda s, h, gc: (s, h, 0, 0))
    o, s1 = pl.pallas_call(
        _ret_sample_body,
        grid_spec=pltpu.PrefetchScalarGridSpec(
            num_scalar_prefetch=1,
            grid=(db, R_HEADS),
            in_specs=[pl.BlockSpec((None, rows, R_DK), lambda s, h, gc: (s, 0, h)),
                      pl.BlockSpec((None, rows, R_DK), lambda s, h, gc: (s, 0, R_Q // R_DK + h)),
                      pl.BlockSpec((None, rows, R_DV), lambda s, h, gc: (s, 0, 2 * R_Q // R_DV + h)),
                      pl.BlockSpec((None, rows, R_DV), lambda s, h, gc: (s, 0, (2 * R_Q + R_V) // R_DV + h)),
                      st, tab(rows), tab(R_DK), tab(R_DK)],
            out_specs=[pl.BlockSpec((None, rows, R_DV), lambda s, h, gc: (s, 0, h)), st]),
        out_shape=[jax.ShapeDtypeStruct((db, rows, R_V), f32),
                   jax.ShapeDtypeStruct(state.shape, f32)],
        compiler_params=_params(("arbitrary", "arbitrary"), 16),
        name="retention_sample",
    )(g_c, zp, zp, zp, zp, state, decay, q_dec, k_dec)
    return o[:, :ts].reshape(db * ts, R_V), s1


PEER_CELLS = [(a, b) for a in range(PEER_TOPK) for b in range(PEER_TOPK) if (a + 1) * (b + 1) <= PEER_TOPK]
PEER_NCELL = len(PEER_CELLS)
PEER_NCELL_PAD = -(-PEER_NCELL // V7X_SUBLANES) * V7X_SUBLANES


def _peer_cell_tables():
    p0 = np.zeros((PEER_NCELL_PAD, PEER_TOPK), np.float32)
    p1 = np.zeros((PEER_NCELL_PAD, PEER_TOPK), np.float32)
    for x, (a, b) in enumerate(PEER_CELLS):
        p0[x, a] = 1.0
        p1[x, b] = 1.0
    return jnp.asarray(p0), jnp.asarray(p1), jnp.asarray(p0.T.copy())


def _top16(s):
    nk = s.shape[0]
    rows = lax.broadcasted_iota(i32, s.shape, 0)
    vrow = lax.broadcasted_iota(i32, (PEER_TOPK, s.shape[1]), 0)
    rk = jnp.full(s.shape, float(PEER_TOPK), f32)
    vals = jnp.zeros((PEER_TOPK, s.shape[1]), f32)
    for r in range(PEER_TOPK):
        mx = jnp.max(s, axis=0, keepdims=True)
        first = jnp.min(jnp.where(s == mx, rows, nk), axis=0, keepdims=True)
        pick = rows == first
        rk = jnp.where(pick, float(r), rk)
        s = jnp.where(pick, -jnp.inf, s)
        vals = jnp.where(vrow == r, mx, vals)
    return vals, rk


def _gather_rows(p, v):
    return jnp.dot(p, v, preferred_element_type=f32, precision=lax.Precision.HIGHEST)


def _peer_query_body(x_ref, sh_ref, sc_ref, wq_ref, o_ref):
    hmod = _modulate(x_ref[...], sh_ref[...], sc_ref[...]).astype(bf16)
    nk = PEER_HALF
    for c in range(0, wq_ref.shape[1], 4 * nk):
        r = jnp.dot(hmod, wq_ref[:, c:c + 4 * nk], preferred_element_type=f32)
        for k in range(4):
            o_ref[c // nk + k] = r[:, k * nk:(k + 1) * nk].astype(bf16)


def _peer_query(x, shift, scale, wq, tm, name):
    m, d = x.shape
    nset = 2 * PEER_HEADS
    row_of = lambda i: i
    return pl.pallas_call(
        _peer_query_body,
        grid=(m // tm,),
        in_specs=[pl.BlockSpec((tm, d), lambda i: (i, 0)), shift.spec(row_of), scale.spec(row_of),
                  pl.BlockSpec(wq.shape, lambda i: (0, 0))],
        out_specs=pl.BlockSpec((nset, tm, PEER_HALF), lambda i: (0, i, 0)),
        out_shape=jax.ShapeDtypeStruct((nset, m, PEER_HALF), bf16),
        compiler_params=_params(("arbitrary",), 32),
        name=name,
    )(x, shift.arr, scale.arr, wq)


def _pack_rows(x):
    return pltpu.bitcast(x.astype(bf16), i32)


def _unpack_rows(w):
    return pltpu.bitcast(w, bf16)


def _dup_words(x):
    hi = lax.bitcast_convert_type(x.astype(bf16).astype(f32), jnp.uint32)
    return lax.bitcast_convert_type(hi | (hi >> 16), i32)


def _peer_route_body(q_ref, keys_ref, p0_ref, p1_ref, p0t_ref, r1_ref, e1_ref, c0_ref, ez_ref):
    G = V7X_SUBLANES
    ngrp = PEER_NCELL_PAD // G
    sub = lax.broadcasted_iota(i32, (G, q_ref.shape[1]), 0)

    def head(h, carry):
        s0 = lax.dot_general(keys_ref[2 * h], q_ref[2 * h], NT_DIMS, preferred_element_type=f32)
        s1 = lax.dot_general(keys_ref[2 * h + 1], q_ref[2 * h + 1], NT_DIMS, preferred_element_type=f32)
        v0, rk0 = _top16(s0)
        v1, rk1 = _top16(s1)
        cand = _gather_rows(p0_ref[...], v0) + _gather_rows(p1_ref[...], v1)
        groups = [cand[g * G:(g + 1) * G] for g in range(ngrp)]
        counts = [jnp.zeros(groups[0].shape, f32) for _ in range(ngrp)]
        for y in range(PEER_NCELL):
            gy, ry = divmod(y, G)
            cy = groups[gy][ry:ry + 1]
            for g in range(ngrp):
                gt = jnp.where(cy > groups[g], 1.0, 0.0)
                ge = jnp.where(cy >= groups[g], 1.0, 0.0)
                if g < gy:
                    ahead = gt
                elif g > gy:
                    ahead = ge
                else:
                    ahead = jnp.where(sub > ry, ge, gt)
                counts[g] = counts[g] + ahead
        cell = lax.broadcasted_iota(i32, cand.shape, 0)
        selm = jnp.where(cell < PEER_NCELL,
                         jnp.where(jnp.concatenate(counts, axis=0) < float(PEER_TOPK), 1.0, 0.0), 0.0)
        cnt = jnp.dot(p0t_ref[...], selm, preferred_element_type=f32)
        ev0 = jnp.exp(v0 - v0[0:1])
        ev1 = jnp.exp(v1 - v1[0:1])
        z = jnp.sum(selm * _gather_rows(p0_ref[...], ev0) * _gather_rows(p1_ref[...], ev1),
                    axis=0, keepdims=True)
        c0 = jnp.zeros(s0.shape, f32)
        for a in range(PEER_TOPK):
            c0 = jnp.where(rk0 == float(a), cnt[a:a + 1], c0)
        r1_ref[h] = _pack_rows(rk1)
        e1_ref[h] = _pack_rows(jnp.exp(s1 - v1[0:1]))
        c0_ref[h] = _dup_words(c0)
        ez_ref[h] = _dup_words(jnp.exp(s0 - v0[0:1]) / z)
        return carry

    lax.fori_loop(0, PEER_HEADS, head, 0)


def _peer_route(q, keys, tm, name):
    nset, m, half = q.shape
    p0, p1, p0t = _peer_cell_tables()
    full = lambda a: pl.BlockSpec(a.shape, lambda i: (0,) * a.ndim)
    pair = pl.BlockSpec((PEER_HEADS, PEER_NKEYS // 2, tm), lambda i: (0, 0, i))
    dup = pl.BlockSpec((PEER_HEADS, PEER_NKEYS, tm), lambda i: (0, 0, i))
    return pl.pallas_call(
        _peer_route_body,
        grid=(m // tm,),
        in_specs=[pl.BlockSpec((nset, tm, half), lambda i: (0, i, 0)), full(keys), full(p0), full(p1), full(p0t)],
        out_specs=[pair, pair, dup, dup],
        out_shape=[jax.ShapeDtypeStruct((PEER_HEADS, PEER_NKEYS // 2, m), i32)] * 2
        + [jax.ShapeDtypeStruct((PEER_HEADS, PEER_NKEYS, m), i32)] * 2,
        compiler_params=_params(("arbitrary",), 40),
        name=name,
    )(q, keys, p0, p1, p0t)


PEER_TE = V7X_SUBLANES * PEER_NKEYS
PEER_TOKEN_CHUNK = 256


def _peer_expert_body(x_ref, sh_ref, sc_ref, gate_ref, g_ref, b_ref, u_ref, vt_ref,
                      r1_ref, e1_ref, c0_ref, ez_ref, o_ref, h_scr, acc_scr, hs_scr):
    e = pl.program_id(1)
    tn = x_ref.shape[0]
    nk = PEER_NKEYS
    tc = min(tn, PEER_TOKEN_CHUNK)

    @pl.when(e == 0)
    def _():
        h_scr[...] = _modulate(x_ref[...], sh_ref[...], sc_ref[...]).astype(bf16)
        acc_scr[...] = jnp.zeros(acc_scr.shape, f32)

    igrp = pl.ds(pl.multiple_of(e * V7X_SUBLANES, V7X_SUBLANES), V7X_SUBLANES)
    pre = [lax.dot_general(u_ref[...], h_scr[c * tc:(c + 1) * tc, :], NT_DIMS, preferred_element_type=f32)
           for c in range(tn // tc)]
    for c in range(tn // tc):
        for il in range(PEER_TE // nk):
            for lg in range(tc // V7X_LANES):
                ln = slice(c * tc + lg * V7X_LANES, c * tc + (lg + 1) * V7X_LANES)
                terms = []
                for hh in range(PEER_HEADS):
                    crow = _unpack_rows(jnp.broadcast_to(c0_ref[hh, igrp, ln][il:il + 1], (nk // 2, V7X_LANES)))
                    zrow = _unpack_rows(jnp.broadcast_to(ez_ref[hh, igrp, ln][il:il + 1], (nk // 2, V7X_LANES)))
                    r1 = _unpack_rows(r1_ref[hh, :, ln])
                    e1 = _unpack_rows(e1_ref[hh, :, ln])
                    terms.append(jnp.where(r1 < crow, e1 * zrow, jnp.zeros((), bf16)))
                while len(terms) > 1:
                    terms = [terms[k] + terms[k + 1] for k in range(0, len(terms), 2)]
                a = pre[c][il * nk:(il + 1) * nk, lg * V7X_LANES:(lg + 1) * V7X_LANES]
                act = 0.5 * a * (1.0 + lax.erf(a * INV_SQRT2))
                hs_scr[il * nk:(il + 1) * nk, ln] = terms[0] * act.astype(bf16)
        cs = slice(c * tc, (c + 1) * tc)
        acc_scr[:, cs] += jnp.dot(vt_ref[...], hs_scr[:, cs], preferred_element_type=f32)

    @pl.when(e == pl.num_programs(1) - 1)
    def _():
        f = acc_scr[...].T
        y = DN_ALPHA * x_ref[...] + gate_ref[...] * f
        o_ref[...] = _layer_norm(y, g_ref[...], b_ref[...])


def _peer_experts(x, shift, scale, gate, ln_g, ln_b, u, vt, route, tm, name):
    m, d = x.shape
    ne = u.shape[0] // PEER_TE
    row_of = lambda i, e: i
    vec = pl.BlockSpec((1, d), lambda i, e: (0, 0))
    pair = pl.BlockSpec((PEER_HEADS, PEER_NKEYS // 2, tm), lambda i, e: (0, 0, i))
    dup = pl.BlockSpec((PEER_HEADS, PEER_NKEYS, tm), lambda i, e: (0, 0, i))
    return pl.pallas_call(
        _peer_expert_body,
        grid=(m // tm, ne),
        in_specs=[pl.BlockSpec((tm, d), lambda i, e: (i, 0)),
                  shift.spec(row_of), scale.spec(row_of), gate.spec(row_of), vec, vec,
                  pl.BlockSpec((PEER_TE, d), lambda i, e: (e, 0)),
                  pl.BlockSpec((d, PEER_TE), lambda i, e: (0, e)),
                  pair, pair, dup, dup],
        out_specs=pl.BlockSpec((tm, d), lambda i, e: (i, 0)),
        out_shape=jax.ShapeDtypeStruct((m, d), f32),
        scratch_shapes=[pltpu.VMEM((tm, d), bf16), pltpu.VMEM((d, tm), f32), pltpu.VMEM((PEER_TE, tm), bf16)],
        compiler_params=_params(("arbitrary", "arbitrary"), 48),
        name=name,
    )(x, shift.arr, scale.arr, gate.arr, ln_g.reshape(1, d), ln_b.reshape(1, d), u, vt, *route)


def kernel(x_prompt, x_sample, cache_k, cache_v, state_ret, page_table, c_prompt, c_sample, w_ada, b_ada,
           ln_g, ln_b, w_qkv_attn, w_o_attn, w_in_ret, w_o_ret, w_q_peer, keys_peer, u_peer, v_peer):
    bp, tp, d = x_prompt.shape
    bs, ts, _ = x_sample.shape
    n_p, n_s = bp * tp, bs * ts
    tm_p = 512
    depth = w_ada.shape[0]

    c_all = jnp.concatenate([c_prompt, c_sample], axis=0)
    pad = (-c_all.shape[0]) % V7X_SUBLANES
    ada = _ada(jnp.pad(c_all, ((0, pad), (0, 0))), w_ada, b_ada)

    slopes = 2.0 ** (-8.0 * jnp.arange(1, N_HEADS + 1, dtype=f32) / N_HEADS)
    ck_t = cache_k.transpose(0, 1, 3, 4, 2)
    cv_t = cache_v.transpose(0, 1, 3, 4, 2)

    yp = x_prompt.reshape(n_p, d)
    ys = x_sample.reshape(n_s, d)
    kp_l, vp_l, ks_l, vs_l, sp_l, ss_l = [], [], [], [], [], []
    for i in range(depth):
        mods_p = [_Mod(ada[i, :bp, c * d:(c + 1) * d], tm_p, tp) for c in range(6)]
        mods_s = [_Mod(ada[i, bp:bp + bs, c * d:(c + 1) * d], n_s, ts) for c in range(6)]
        j = i // 2
        if i % 2 == 0:
            w = w_qkv_attn[j].astype(bf16)
            wo = w_o_attn[j].astype(bf16)
            qh, kh, vt_h, kp_t, vp_t, kmean = _qkv_heads(yp, mods_p[0], mods_p[1], w, bp, tp, tm_p, "qkv_prompt")
            ap = _moba_prompt(qh, kh, vt_h, kmean, slopes)
            qs, _, kss, _, vss, _ = _qkv(ys, mods_s[0], mods_s[1], w, n_s, "qkv_sample")
            to_t = lambda a: a.reshape(bs, ts, N_HEADS, HEAD_DIM).transpose(0, 2, 3, 1)
            q_t = to_t(qs)
            scores = _sample_block_scores(ck_t, j, page_table, q_t)
            sel = _sample_topk(scores.reshape(bs, scores.shape[1], N_HEADS * ts))
            sel_flat = sel.reshape(bs, MOBA_TOPK, N_HEADS, ts).transpose(0, 2, 3, 1).reshape(-1)
            a_t = _sample_attention(sel_flat, page_table, slopes, q_t, to_t(kss), to_t(vss), ck_t, cv_t, j)
            a_s = a_t.transpose(0, 3, 1, 2).reshape(n_s, d)
            kp_l.append(kp_t.transpose(0, 3, 1, 2))
            vp_l.append(vp_t.transpose(0, 3, 1, 2))
            ks_l.append(kss.reshape(bs, ts, N_HEADS, HEAD_DIM))
            vs_l.append(vss.reshape(bs, ts, N_HEADS, HEAD_DIM))
        else:
            w = w_in_ret[j].astype(bf16)
            wo = w_o_ret[j].astype(bf16)
            zp = _mod_matmul(yp, mods_p[0], mods_p[1], w, tm_p, 2048, "ret_in_prompt")
            ap, s_fin = _retention_prompt(zp, bp, tp)
            zs = _mod_matmul(ys, mods_s[0], mods_s[1], w, n_s, 2048, "ret_in_sample")
            a_s, s_new = _retention_sample(zs, state_ret[j], bs, ts)
            sp_l.append(s_fin)
            ss_l.append(s_new)
        yp = _proj_res_ln(ap, wo, yp, mods_p[2], ln_g[i, 0], ln_b[i, 0], tm_p, "mixer_out_prompt")
        ys = _proj_res_ln(a_s, wo, ys, mods_s[2], ln_g[i, 0], ln_b[i, 0], n_s, "mixer_out_sample")

        wq = w_q_peer[i].astype(bf16)
        keys = keys_peer[i].reshape(2 * PEER_HEADS, PEER_NKEYS, PEER_HALF).astype(bf16)
        u = u_peer[i].astype(bf16)
        vt = v_peer[i].astype(bf16).T
        route_p = _peer_route(_peer_query(yp, mods_p[3], mods_p[4], wq, tm_p, "peer_query_prompt"),
                              keys, 2 * V7X_LANES, "peer_route_prompt")
        yp = _peer_experts(yp, mods_p[3], mods_p[4], mods_p[5], ln_g[i, 1], ln_b[i, 1], u, vt, route_p,
                           tm_p, "peer_experts_prompt")
        route_s = _peer_route(_peer_query(ys, mods_s[3], mods_s[4], wq, n_s, "peer_query_sample"),
                              keys, n_s, "peer_route_sample")
        ys = _peer_experts(ys, mods_s[3], mods_s[4], mods_s[5], ln_g[i, 1], ln_b[i, 1], u, vt, route_s,
                           n_s, "peer_experts_sample")

    return (yp.reshape(bp, tp, d), ys.reshape(bs, ts, d),
            jnp.stack(kp_l), jnp.stack(vp_l), jnp.stack(ks_l), jnp.stack(vs_l),
            jnp.stack(sp_l), jnp.stack(ss_l))
```

```python
import functools

import jax
import jax.numpy as jnp
import numpy as np
from jax import lax
from jax.experimental import pallas as pl
from jax.experimental.pallas import tpu as pltpu

f32 = jnp.float32
bf16 = jnp.bfloat16
i32 = jnp.int32

D_MODEL = 1024
DEPTH = 2
PAGE_SIZE = 128
N_HEADS = 16
HEAD_DIM = D_MODEL // N_HEADS
MOBA_BLOCK = 256
MOBA_TOPK = 3
R_HEADS = 4
R_DK = D_MODEL // R_HEADS
R_DV = 2 * R_DK
R_Q = R_HEADS * R_DK
R_V = R_HEADS * R_DV
R_IN = 2 * R_Q + 2 * R_V
R_CHUNK = 128
PEER_HEADS = 8
PEER_NKEYS = 128
PEER_EXPERTS = PEER_NKEYS * PEER_NKEYS
PEER_HALF = 128
PEER_TOPK = 16
DN_ALPHA = (2 * DEPTH) ** 0.25
LN_EPS = 1e-5

V7X_LANES = 128
V7X_SUBLANES = 8
V7X_VMEM_BYTES = 64 * 1024 * 1024

NEG = -1e30
INV_SQRT2 = 0.7071067811865476

NT_DIMS = (((1,), (1,)), ((), ()))
TN_DIMS = (((0,), (0,)), ((), ()))


def _vmem(mib):
    assert mib * 1024 * 1024 < V7X_VMEM_BYTES
    return mib * 1024 * 1024


def _params(sem, mib):
    return pltpu.CompilerParams(dimension_semantics=sem, vmem_limit_bytes=_vmem(mib))


def _modulate(x, shift, scale):
    return x * (1.0 + scale) + shift


def _layer_norm(y, g, b):
    mu = jnp.mean(y, axis=-1, keepdims=True)
    yc = y - mu
    var = jnp.mean(yc * yc, axis=-1, keepdims=True)
    return yc * lax.rsqrt(var + LN_EPS) * g + b


class _Mod:
    def __init__(self, vec, tm, rows_per_group):
        self.tm = tm
        if rows_per_group % tm == 0:
            self.arr = vec[:, None, :]
            self.tiles_per_group = rows_per_group // tm
            self.per_row = False
        else:
            self.arr = jnp.repeat(vec, rows_per_group, axis=0)
            self.per_row = True

    def spec(self, row_of):
        d = self.arr.shape[-1]
        if self.per_row:
            return pl.BlockSpec((self.tm, d), lambda *g: (row_of(*g), 0))
        tpg = self.tiles_per_group
        return pl.BlockSpec((None, 1, d), lambda *g: (row_of(*g) // tpg, 0, 0))


def _ada_body(c_ref, w_ref, b_ref, o_ref):
    c = c_ref[...]
    a = c * (1.0 / (1.0 + jnp.exp(-c)))
    o_ref[...] = jnp.dot(a, w_ref[...], preferred_element_type=f32,
                         precision=lax.Precision.HIGHEST) + b_ref[...]


def _ada(c_all, w_ada, b_ada):
    depth, d, n6 = w_ada.shape
    rows = c_all.shape[0]
    tn = 1024
    return pl.pallas_call(
        _ada_body,
        grid=(depth, n6 // tn),
        in_specs=[pl.BlockSpec((rows, d), lambda l, j: (0, 0)),
                  pl.BlockSpec((None, d, tn), lambda l, j: (l, 0, j)),
                  pl.BlockSpec((None, 1, tn), lambda l, j: (l, 0, j))],
        out_specs=pl.BlockSpec((None, rows, tn), lambda l, j: (l, 0, j)),
        out_shape=jax.ShapeDtypeStruct((depth, rows, n6), f32),
        compiler_params=_params(("arbitrary", "arbitrary"), 24),
        name="ada",
    )(c_all, w_ada, b_ada.reshape(depth, 1, n6))


def _mod_mm_body(x_ref, sh_ref, sc_ref, w_ref, o_ref):
    h = _modulate(x_ref[...], sh_ref[...], sc_ref[...]).astype(bf16)
    n = o_ref.shape[1]
    for c in range(0, n, 512):
        o_ref[:, c:c + 512] = jnp.dot(h, w_ref[:, c:c + 512], preferred_element_type=f32)


def _mod_matmul(x, shift, scale, w, tm, tn, name):
    m, d = x.shape
    n = w.shape[1]
    row_of = lambda j, i: i
    return pl.pallas_call(
        _mod_mm_body,
        grid=(n // tn, m // tm),
        in_specs=[pl.BlockSpec((tm, d), lambda j, i: (i, 0)),
                  shift.spec(row_of), scale.spec(row_of),
                  pl.BlockSpec((d, tn), lambda j, i: (0, j))],
        out_specs=pl.BlockSpec((tm, tn), lambda j, i: (i, j)),
        out_shape=jax.ShapeDtypeStruct((m, n), f32),
        compiler_params=_params(("arbitrary", "arbitrary"), 40),
        name=name,
    )(x, shift.arr, scale.arr, w)


def _qkv_body(x_ref, sh_ref, sc_ref, w_ref, q_ref, qb_ref, k_ref, kb_ref, v_ref, vb_ref):
    h = _modulate(x_ref[...], sh_ref[...], sc_ref[...]).astype(bf16)
    d = D_MODEL
    for c in range(0, d, 512):
        q = jnp.dot(h, w_ref[:, c:c + 512], preferred_element_type=f32)
        q_ref[:, c:c + 512] = q
        qb_ref[:, c:c + 512] = (q * (HEAD_DIM ** -0.5)).astype(bf16)
        k = jnp.dot(h, w_ref[:, d + c:d + c + 512], preferred_element_type=f32)
        k_ref[:, c:c + 512] = k
        kb_ref[:, c:c + 512] = k.astype(bf16)
        v = jnp.dot(h, w_ref[:, 2 * d + c:2 * d + c + 512], preferred_element_type=f32)
        v_ref[:, c:c + 512] = v
        vb_ref[:, c:c + 512] = v.astype(bf16)


def _qkv(x, shift, scale, w, tm, name):
    m, d = x.shape
    row_of = lambda i: i
    blk = pl.BlockSpec((tm, d), lambda i: (i, 0))
    return pl.pallas_call(
        _qkv_body,
        grid=(m // tm,),
        in_specs=[blk, shift.spec(row_of), scale.spec(row_of),
                  pl.BlockSpec((d, 3 * d), lambda i: (0, 0))],
        out_specs=[blk] * 6,
        out_shape=[jax.ShapeDtypeStruct((m, d), t) for t in (f32, bf16, f32, bf16, f32, bf16)],
        compiler_params=_params(("arbitrary",), 48),
        name=name,
    )(x, shift.arr, scale.arr, w)


def _qkv_heads_body(x_ref, sh_ref, sc_ref, w_ref, qh_ref, kh_ref, vt_ref, kt_ref, vtf_ref, km_ref):
    h = _modulate(x_ref[...], sh_ref[...], sc_ref[...]).astype(bf16)
    d, dh, L = D_MODEL, HEAD_DIM, MOBA_BLOCK
    tm = x_ref.shape[0]
    cw = 512
    for c in range(0, d, cw):
        q = jnp.dot(h, w_ref[:, c:c + cw], preferred_element_type=f32) * (dh ** -0.5)
        k = jnp.dot(h, w_ref[:, d + c:d + c + cw], preferred_element_type=f32)
        v = jnp.dot(h, w_ref[:, 2 * d + c:2 * d + c + cw], preferred_element_type=f32)
        k_t = k.T
        v_t = v.T
        for blk in range(tm // L):
            km_ref[blk, :, c:c + cw] = jnp.sum(k[blk * L:(blk + 1) * L], axis=0, keepdims=True) * (1.0 / L)
        for hh in range(cw // dh):
            head = c // dh + hh
            cols = slice(hh * dh, (hh + 1) * dh)
            qh_ref[head] = q[:, cols].astype(bf16)
            kt_ref[head] = k_t[cols, :]
            vtf_ref[head] = v_t[cols, :]
            for blk in range(tm // L):
                kh_ref[head, blk] = k[blk * L:(blk + 1) * L, cols].astype(bf16)
                vt_ref[head, blk] = v_t[cols, blk * L:(blk + 1) * L].astype(bf16)


def _qkv_heads(x, shift, scale, w, b, t, tm, name):
    m, d = x.shape
    nh, dh, L = N_HEADS, HEAD_DIM, MOBA_BLOCK
    nb, tpb, bpt = t // L, t // tm, tm // L
    row_of = lambda bi, i: bi * tpb + i
    return pl.pallas_call(
        _qkv_heads_body,
        grid=(b, tpb),
        in_specs=[pl.BlockSpec((tm, d), lambda bi, i: (bi * tpb + i, 0)),
                  shift.spec(row_of), scale.spec(row_of),
                  pl.BlockSpec((d, 3 * d), lambda bi, i: (0, 0))],
        out_specs=[pl.BlockSpec((None, nh, tm, dh), lambda bi, i: (bi, 0, i, 0)),
                   pl.BlockSpec((None, nh, bpt, L, dh), lambda bi, i: (bi, 0, i, 0, 0)),
                   pl.BlockSpec((None, nh, bpt, dh, L), lambda bi, i: (bi, 0, i, 0, 0)),
                   pl.BlockSpec((None, nh, dh, tm), lambda bi, i: (bi, 0, 0, i)),
                   pl.BlockSpec((None, nh, dh, tm), lambda bi, i: (bi, 0, 0, i)),
                   pl.BlockSpec((None, bpt, 1, d), lambda bi, i: (bi, i, 0, 0))],
        out_shape=[jax.ShapeDtypeStruct((b, nh, t, dh), bf16),
                   jax.ShapeDtypeStruct((b, nh, nb, L, dh), bf16),
                   jax.ShapeDtypeStruct((b, nh, nb, dh, L), bf16),
                   jax.ShapeDtypeStruct((b, nh, dh, t), f32),
                   jax.ShapeDtypeStruct((b, nh, dh, t), f32),
                   jax.ShapeDtypeStruct((b, nb, 1, d), f32)],
        compiler_params=_params(("arbitrary", "arbitrary"), 56),
        name=name,
    )(x, shift.arr, scale.arr, w)


def _proj_ln_body(a_ref, w_ref, x_ref, gate_ref, g_ref, b_ref, o_ref):
    f = jnp.dot(a_ref[...].astype(bf16), w_ref[...], preferred_element_type=f32)
    y = DN_ALPHA * x_ref[...] + gate_ref[...] * f
    o_ref[...] = _layer_norm(y, g_ref[...], b_ref[...])


def _proj_res_ln(a, w, x, gate, ln_g, ln_b, tm, name):
    m, k = a.shape
    d = x.shape[1]
    row_of = lambda i: i
    vec = pl.BlockSpec((1, d), lambda i: (0, 0))
    return pl.pallas_call(
        _proj_ln_body,
        grid=(m // tm,),
        in_specs=[pl.BlockSpec((tm, k), lambda i: (i, 0)),
                  pl.BlockSpec((k, d), lambda i: (0, 0)),
                  pl.BlockSpec((tm, d), lambda i: (i, 0)),
                  gate.spec(row_of), vec, vec],
        out_specs=pl.BlockSpec((tm, d), lambda i: (i, 0)),
        out_shape=jax.ShapeDtypeStruct((m, d), f32),
        compiler_params=_params(("arbitrary",), 40),
        name=name,
    )(a, w, x, gate.arr, ln_g.reshape(1, d), ln_b.reshape(1, d))


def _kmean_body(k_ref, o_ref):
    o_ref[...] = jnp.sum(k_ref[...], axis=0, keepdims=True) * (1.0 / MOBA_BLOCK)


def _block_means(k):
    m, d = k.shape
    nb = m // MOBA_BLOCK
    return pl.pallas_call(
        _kmean_body,
        grid=(nb,),
        in_specs=[pl.BlockSpec((MOBA_BLOCK, d), lambda i: (i, 0))],
        out_specs=pl.BlockSpec((None, 1, d), lambda i: (i, 0, 0)),
        out_shape=jax.ShapeDtypeStruct((nb, 1, d), f32),
        compiler_params=_params(("arbitrary",), 16),
        name="moba_block_means",
    )(k)


MOBA_HEADS_PER_STEP = 8


def _moba_body(sl_ref, q_ref, k_ref, vt_ref, km_ref, o_ref, sel_ref, base_ref):
    hg = q_ref.shape[0]
    dh = q_ref.shape[2]
    h0 = pl.program_id(1) * hg
    qt = pl.program_id(2)
    L = MOBA_BLOCK
    nb = km_ref.shape[0]
    ki = lax.broadcasted_iota(i32, (L, L), 0)
    qi = lax.broadcasted_iota(i32, (L, L), 1)

    @pl.when(qt == 0)
    def _():
        rel = (qi - ki).astype(f32)
        for hh in range(hg):
            base_ref[hh] = (-sl_ref[h0 + hh]) * rel

    state = []
    for hh in range(hg):
        q = q_ref[hh]
        km = km_ref[:, 0, hh * dh:(hh + 1) * dh]
        sc = lax.dot_general(km.astype(bf16), q, NT_DIMS, preferred_element_type=f32)
        rows = lax.broadcasted_iota(i32, sc.shape, 0)
        sc = jnp.where(rows < qt, sc, -jnp.inf)
        sel = jnp.zeros(sc.shape, f32)
        for r in range(MOBA_TOPK):
            mx = jnp.max(sc, axis=0, keepdims=True)
            first = jnp.min(jnp.where(sc == mx, rows, nb), axis=0, keepdims=True)
            pick = rows == first
            slot_ok = jnp.where(qt > r, 1.0, 0.0)
            sel = jnp.maximum(sel, jnp.where(pick, slot_ok, 0.0))
            sc = jnp.where(pick, -jnp.inf, sc)
        sel_ref[hh] = sel
        s = lax.dot_general(k_ref[hh, qt], q, NT_DIMS, preferred_element_type=f32) + base_ref[hh]
        s = jnp.where(ki <= qi, s, NEG)
        m0 = jnp.max(s, axis=0, keepdims=True)
        p = jnp.exp(s - m0)
        l0 = jnp.sum(p, axis=0, keepdims=True)
        state += [m0, l0, jnp.dot(vt_ref[hh, qt], p.astype(bf16), preferred_element_type=f32)]

    def step(mb, carry):
        scores = [lax.dot_general(k_ref[hh, mb], q_ref[hh], NT_DIMS, preferred_element_type=f32)
                  for hh in range(hg)]
        out, pv = [], []
        for hh in range(hg):
            m, l, acc = carry[3 * hh:3 * hh + 3]
            cm = (qt - mb).astype(f32) * ((-sl_ref[h0 + hh]) * L)
            addrow = jnp.where(sel_ref[hh, pl.ds(mb, 1), :] > 0.0, cm, NEG)
            s = scores[hh] + base_ref[hh] + addrow
            m_new = jnp.maximum(m, jnp.max(s, axis=0, keepdims=True))
            alpha = jnp.exp(m - m_new)
            p = jnp.exp(s - m_new)
            l = l * alpha + jnp.sum(p, axis=0, keepdims=True)
            out += [m_new, l, acc * alpha]
            pv.append(p.astype(bf16))
        for hh in range(hg):
            out[3 * hh + 2] = out[3 * hh + 2] + jnp.dot(vt_ref[hh, mb], pv[hh], preferred_element_type=f32)
        return tuple(out)

    final = lax.fori_loop(0, qt, step, tuple(state))
    out_t = jnp.concatenate([final[3 * hh + 2] / final[3 * hh + 1] for hh in range(hg)], axis=0)
    o_ref[...] = out_t.T


def _moba_prompt(qh, kh, vt, km, slopes):
    b, nh, t, dh = qh.shape
    L, hg = MOBA_BLOCK, MOBA_HEADS_PER_STEP
    nb = t // L
    return pl.pallas_call(
        _moba_body,
        grid_spec=pltpu.PrefetchScalarGridSpec(
            num_scalar_prefetch=1,
            grid=(b, nh // hg, nb),
            in_specs=[pl.BlockSpec((None, hg, L, dh), lambda bi, g, i, sl: (bi, g, i, 0)),
                      pl.BlockSpec((None, hg, nb, L, dh), lambda bi, g, i, sl: (bi, g, 0, 0, 0)),
                      pl.BlockSpec((None, hg, nb, dh, L), lambda bi, g, i, sl: (bi, g, 0, 0, 0)),
                      pl.BlockSpec((None, nb, 1, hg * dh), lambda bi, g, i, sl: (bi, 0, 0, g))],
            out_specs=pl.BlockSpec((L, hg * dh), lambda bi, g, i, sl: (bi * nb + i, g)),
            scratch_shapes=[pltpu.VMEM((hg, nb, L), f32), pltpu.VMEM((hg, L, L), f32)]),
        out_shape=jax.ShapeDtypeStruct((b * t, nh * dh), f32),
        compiler_params=_params(("arbitrary", "arbitrary", "arbitrary"), 40),
        name="moba_prompt_attn",
    )(slopes, qh, kh, vt, km)


SAMPLE_PAGES_PER_STEP = 8


def _sample_scores_body(pt_ref, *refs):
    npg = SAMPLE_PAGES_PER_STEP
    k_refs, q_ref, o_ref = refs[:npg], refs[npg], refs[npg + 1]
    j = pl.program_id(1)
    ppb = MOBA_BLOCK // PAGE_SIZE
    for bb in range(npg // ppb):
        tot = k_refs[bb * ppb][...]
        for pg in range(1, ppb):
            tot = tot + k_refs[bb * ppb + pg][...]
        kmean = jnp.sum(tot, axis=-1, keepdims=True) * (1.0 / MOBA_BLOCK)
        o_ref[j * (npg // ppb) + bb] = jnp.sum(kmean * q_ref[...], axis=1)


def _sample_block_scores(ck_t, layer, page_table, q_t):
    db, n_pages = page_table.shape
    _, _, nh, dh, pg = ck_t.shape
    tn = q_t.shape[-1]
    npg = SAMPLE_PAGES_PER_STEP
    nbp = n_pages * PAGE_SIZE // MOBA_BLOCK

    def page_spec(o):
        return pl.BlockSpec((None, None, nh, dh, pg), lambda s, j, pt: (layer, pt[s, j * npg + o], 0, 0, 0))

    return pl.pallas_call(
        _sample_scores_body,
        grid_spec=pltpu.PrefetchScalarGridSpec(
            num_scalar_prefetch=1,
            grid=(db, n_pages // npg),
            in_specs=[page_spec(o) for o in range(npg)]
            + [pl.BlockSpec((None, nh, dh, tn), lambda s, j, pt: (s, 0, 0, 0))],
            out_specs=pl.BlockSpec((None, nbp, nh, tn), lambda s, j, pt: (s, 0, 0, 0))),
        out_shape=jax.ShapeDtypeStruct((db, nbp, nh, tn), f32),
        compiler_params=_params(("arbitrary", "arbitrary"), 32),
        name="moba_sample_block_scores",
    )(page_table, *([ck_t] * npg), q_t)


def _sample_topk_body(s_ref, o_ref):
    s = s_ref[...]
    nbp = s.shape[0]
    rows = lax.broadcasted_iota(i32, s.shape, 0)
    for r in range(MOBA_TOPK):
        mx = jnp.max(s, axis=0, keepdims=True)
        first = jnp.min(jnp.where(s == mx, rows, nbp), axis=0, keepdims=True)
        o_ref[r:r + 1, :] = jnp.minimum(first, nbp - 1)
        s = jnp.where(rows == first, -jnp.inf, s)


def _sample_topk(scores):
    db, nbp, c = scores.shape
    return pl.pallas_call(
        _sample_topk_body,
        grid=(db,),
        in_specs=[pl.BlockSpec((None, nbp, c), lambda s: (s, 0, 0))],
        out_specs=pl.BlockSpec((None, MOBA_TOPK, c), lambda s: (s, 0, 0)),
        out_shape=jax.ShapeDtypeStruct((db, MOBA_TOPK, c), i32),
        compiler_params=_params(("arbitrary",), 16),
        name="moba_sample_topk",
    )(scores)


def _sample_page_copies(sel_ref, pt_ref, ck_ref, cv_ref, kbuf, vbuf, sem, step, slot, layer, tn):
    ppb = MOBA_BLOCK // PAGE_SIZE
    s = step // N_HEADS
    h = step % N_HEADS
    out = []
    for qi in range(tn):
        for sl in range(MOBA_TOPK):
            blk = sel_ref[(step * tn + qi) * MOBA_TOPK + sl]
            for pg in range(ppb):
                page = pt_ref[s, blk * ppb + pg]
                idx = (qi * MOBA_TOPK + sl) * ppb + pg
                out.append(pltpu.make_async_copy(ck_ref.at[layer, page, h], kbuf.at[slot, idx], sem.at[slot]))
                out.append(pltpu.make_async_copy(cv_ref.at[layer, page, h], vbuf.at[slot, idx], sem.at[slot]))
    return out


def _sample_attn_body(sel_ref, pt_ref, sl_ref, q_ref, kn_ref, vn_ref, ck_ref, cv_ref, o_ref,
                      kbuf, vbuf, sem, *, layer, past_len):
    tn = q_ref.shape[1]
    ppb = MOBA_BLOCK // PAGE_SIZE
    n = pl.program_id(0)
    nsteps = pl.num_programs(0)
    slot = n % 2
    args = (sel_ref, pt_ref, ck_ref, cv_ref, kbuf, vbuf, sem)

    @pl.when(n == 0)
    def _():
        for c in _sample_page_copies(*args, 0, 0, layer, tn):
            c.start()

    @pl.when(n + 1 < nsteps)
    def _():
        for c in _sample_page_copies(*args, n + 1, 1 - slot, layer, tn):
            c.start()

    for c in _sample_page_copies(*args, n, slot, layer, tn):
        c.wait()

    h = n % N_HEADS
    slope = sl_ref[h]
    lane = lax.broadcasted_iota(i32, (1, PAGE_SIZE), 1).astype(f32)
    jn = lax.broadcasted_iota(i32, (1, tn), 1)
    kn = kn_ref[...]
    vn = vn_ref[...]
    for qi in range(tn):
        qcol = q_ref[:, qi:qi + 1] * (HEAD_DIM ** -0.5)
        t_q = past_len + qi
        s_rows = []
        for sl in range(MOBA_TOPK):
            blk = sel_ref[(n * tn + qi) * MOBA_TOPK + sl]
            for pg in range(ppb):
                idx = (qi * MOBA_TOPK + sl) * ppb + pg
                sc = jnp.sum(kbuf[slot, idx] * qcol, axis=0, keepdims=True)
                dist0 = (t_q - (blk * MOBA_BLOCK + pg * PAGE_SIZE)).astype(f32)
                s_rows.append(sc - slope * (dist0 - lane))
        s_own = jnp.sum(kn * qcol, axis=0, keepdims=True)
        s_own = jnp.where(jn <= qi, s_own - slope * (qi - jn).astype(f32), NEG)
        m = jnp.max(s_own, axis=1, keepdims=True)
        for sc in s_rows:
            m = jnp.maximum(m, jnp.max(sc, axis=1, keepdims=True))
        p_own = jnp.exp(s_own - m)
        l = jnp.sum(p_own, axis=1, keepdims=True)
        acc = jnp.zeros((HEAD_DIM, PAGE_SIZE), f32)
        for r, sc in enumerate(s_rows):
            p = jnp.exp(sc - m)
            l = l + jnp.sum(p, axis=1, keepdims=True)
            acc = acc + vbuf[slot, qi * MOBA_TOPK * ppb + r] * p
        o = jnp.sum(acc, axis=1, keepdims=True) + jnp.sum(vn * p_own, axis=1, keepdims=True)
        o_ref[:, qi:qi + 1] = o / l


def _sample_attention(sel_flat, page_table, slopes, q_t, kn_t, vn_t, ck_t, cv_t, layer):
    db, nh, dh, tn = q_t.shape
    past_len = page_table.shape[1] * PAGE_SIZE
    assert past_len % MOBA_BLOCK == 0
    npg = tn * MOBA_TOPK * (MOBA_BLOCK // PAGE_SIZE)
    blk = pl.BlockSpec((None, None, dh, tn), lambda n, *_: (n // N_HEADS, n % N_HEADS, 0, 0))
    return pl.pallas_call(
        functools.partial(_sample_attn_body, layer=layer, past_len=past_len),
        grid_spec=pltpu.PrefetchScalarGridSpec(
            num_scalar_prefetch=3,
            grid=(db * nh,),
            in_specs=[blk, blk, blk,
                      pl.BlockSpec(memory_space=pl.ANY), pl.BlockSpec(memory_space=pl.ANY)],
            out_specs=blk,
            scratch_shapes=[pltpu.VMEM((2, npg, dh, PAGE_SIZE), f32),
                            pltpu.VMEM((2, npg, dh, PAGE_SIZE), f32),
                            pltpu.SemaphoreType.DMA((2,))]),
        out_shape=jax.ShapeDtypeStruct((db, nh, dh, tn), f32),
        compiler_params=_params(("arbitrary",), 16),
        name="moba_sample_attn",
    )(sel_flat, page_table, slopes, q_t, kn_t, vn_t, ck_t, cv_t)


def _retention_tables(c_len, rows):
    lg = jnp.log(1.0 - 2.0 ** (-5.0 - jnp.arange(R_HEADS, dtype=f32)))
    idx = jnp.arange(rows, dtype=f32)
    live = idx < c_len
    diff = idx[:, None] - idx[None, :]
    decay = jnp.where((diff >= 0) & live[:, None] & live[None, :],
                      jnp.exp(lg[:, None, None] * jnp.maximum(diff, 0.0)), 0.0)
    q_dec = jnp.where(live, jnp.exp(lg[:, None] * (idx + 1.0)), 0.0)
    k_dec = jnp.where(live, jnp.exp(lg[:, None] * (c_len - 1.0 - idx)), 0.0)
    g_c = jnp.exp(lg * c_len)
    q_dec = jnp.broadcast_to(q_dec[:, :, None], (R_HEADS, rows, R_DK))
    k_dec = jnp.broadcast_to(k_dec[:, :, None], (R_HEADS, rows, R_DK))
    return decay, q_dec, k_dec, g_c


def _retention_chunk(q, k, v, g, dec, qd, kd, g_c, s_ref):
    k = k * (R_DK ** -0.5)
    vb = v.astype(bf16)
    inner = lax.dot_general(q.astype(bf16), k.astype(bf16), NT_DIMS, preferred_element_type=f32) * dec
    o = jnp.dot(inner.astype(bf16), vb, preferred_element_type=f32)
    o = o + jnp.dot((q * qd).astype(bf16), s_ref[...].astype(bf16), preferred_element_type=f32)
    s_ref[...] = s_ref[...] * g_c + lax.dot_general((k * kd).astype(bf16), vb, TN_DIMS,
                                                    preferred_element_type=f32)
    mu = jnp.mean(o, axis=-1, keepdims=True)
    oc = o - mu
    var = jnp.mean(oc * oc, axis=-1, keepdims=True)
    on = oc * lax.rsqrt(var + LN_EPS)
    return (g * (1.0 / (1.0 + jnp.exp(-g)))) * on


def _ret_prompt_body(gc_ref, q_ref, k_ref, v_ref, g_ref, dec_ref, qd_ref, kd_ref, o_ref, sfin_ref, s_ref):
    h = pl.program_id(1)
    c = pl.program_id(2)

    @pl.when(c == 0)
    def _():
        s_ref[...] = jnp.zeros(s_ref.shape, f32)

    g_c = gc_ref[h]
    C = R_CHUNK
    for cc in range(q_ref.shape[0] // C):
        r = slice(cc * C, (cc + 1) * C)
        o_ref[r, :] = _retention_chunk(q_ref[r, :], k_ref[r, :], v_ref[r, :], g_ref[r, :],
                                       dec_ref[...], qd_ref[...], kd_ref[...], g_c, s_ref)

    @pl.when(c == pl.num_programs(2) - 1)
    def _():
        sfin_ref[...] = s_ref[...]


def _retention_prompt(z, b, t):
    rb = 4 * R_CHUNK
    nrb = t // rb
    decay, q_dec, k_dec, g_c = _retention_tables(R_CHUNK, R_CHUNK)
    row = lambda bi, h, c, gc: bi * nrb + c
    tab = lambda w: pl.BlockSpec((None, R_CHUNK, w), lambda bi, h, c, gc: (h, 0, 0))
    return pl.pallas_call(
        _ret_prompt_body,
        grid_spec=pltpu.PrefetchScalarGridSpec(
            num_scalar_prefetch=1,
            grid=(b, R_HEADS, nrb),
            in_specs=[pl.BlockSpec((rb, R_DK), lambda *g: (row(*g), g[1])),
                      pl.BlockSpec((rb, R_DK), lambda *g: (row(*g), R_Q // R_DK + g[1])),
                      pl.BlockSpec((rb, R_DV), lambda *g: (row(*g), 2 * R_Q // R_DV + g[1])),
                      pl.BlockSpec((rb, R_DV), lambda *g: (row(*g), (2 * R_Q + R_V) // R_DV + g[1])),
                      tab(R_CHUNK), tab(R_DK), tab(R_DK)],
            out_specs=[pl.BlockSpec((rb, R_DV), lambda *g: (row(*g), g[1])),
                       pl.BlockSpec((None, None, R_DK, R_DV), lambda bi, h, c, gc: (bi, h, 0, 0))],
            scratch_shapes=[pltpu.VMEM((R_DK, R_DV), f32)]),
        out_shape=[jax.ShapeDtypeStruct((b * t, R_V), f32),
                   jax.ShapeDtypeStruct((b, R_HEADS, R_DK, R_DV), f32)],
        compiler_params=_params(("arbitrary", "arbitrary", "arbitrary"), 32),
        name="retention_prompt",
    )(g_c, z, z, z, z, decay, q_dec, k_dec)


def _ret_sample_body(gc_ref, q_ref, k_ref, v_ref, g_ref, s0_ref, dec_ref, qd_ref, kd_ref, o_ref, s1_ref):
    h = pl.program_id(1)
    s1_ref[...] = s0_ref[...]
    o_ref[...] = _retention_chunk(q_ref[...], k_ref[...], v_ref[...], g_ref[...],
                                  dec_ref[...], qd_ref[...], kd_ref[...], gc_ref[h], s1_ref)


def _retention_sample(z, state, db, ts):
    rows = V7X_SUBLANES
    assert ts <= rows
    zp = jnp.pad(z.reshape(db, ts, R_IN), ((0, 0), (0, rows - ts), (0, 0)))
    decay, q_dec, k_dec, g_c = _retention_tables(ts, rows)
    tab = lambda w: pl.BlockSpec((None, rows, w), lambda s, h, gc: (h, 0, 0))
    st = pl.BlockSpec((None, None, R_DK, R_DV), lambda s, h, gc: (s, h, 0, 0))
    o, s1 = pl.pallas_call(
        _ret_sample_body,
        grid_spec=pltpu.PrefetchScalarGridSpec(
            num_scalar_prefetch=1,
            grid=(db, R_HEADS),
            in_specs=[pl.BlockSpec((None, rows, R_DK), lambda s, h, gc: (s, 0, h)),
                      pl.BlockSpec((None, rows, R_DK), lambda s, h, gc: (s, 0, R_Q // R_DK + h)),
                      pl.BlockSpec((None, rows, R_DV), lambda s, h, gc: (s, 0, 2 * R_Q // R_DV + h)),
                      pl.BlockSpec((None, rows, R_DV), lambda s, h, gc: (s, 0, (2 * R_Q + R_V) // R_DV + h)),
                      st, tab(rows), tab(R_DK), tab(R_DK)],
            out_specs=[pl.BlockSpec((None, rows, R_DV), lambda s, h, gc: (s, 0, h)), st]),
        out_shape=[jax.ShapeDtypeStruct((db, rows, R_V), f32),
                   jax.ShapeDtypeStruct(state.shape, f32)],
        compiler_params=_params(("arbitrary", "arbitrary"), 16),
        name="retention_sample",
    )(g_c, zp, zp, zp, zp, state, decay, q_dec, k_dec)
    return o[:, :ts].reshape(db * ts, R_V), s1


PEER_CELLS = [(a, b) for a in range(PEER_TOPK) for b in range(PEER_TOPK) if (a + 1) * (b + 1) <= PEER_TOPK]
PEER_NCELL = len(PEER_CELLS)
PEER_NCELL_PAD = -(-PEER_NCELL // V7X_SUBLANES) * V7X_SUBLANES


def _peer_cell_tables():
    p0 = np.zeros((PEER_NCELL_PAD, PEER_TOPK), np.float32)
    p1 = np.zeros((PEER_NCELL_PAD, PEER_TOPK), np.float32)
    for x, (a, b) in enumerate(PEER_CELLS):
        p0[x, a] = 1.0
        p1[x, b] = 1.0
    return jnp.asarray(p0), jnp.asarray(p1), jnp.asarray(p0.T.copy())


def _top16(s, break_ties):
    nk = s.shape[0]
    rows = lax.broadcasted_iota(i32, s.shape, 0)
    vrow = lax.broadcasted_iota(i32, (PEER_TOPK, s.shape[1]), 0)
    rk = jnp.full(s.shape, float(PEER_TOPK), f32)
    vals = jnp.zeros((PEER_TOPK, s.shape[1]), f32)
    for r in range(PEER_TOPK):
        mx = jnp.max(s, axis=0, keepdims=True)
        pick = s == mx
        if break_ties:
            first = jnp.min(jnp.where(pick, rows, nk), axis=0, keepdims=True)
            pick = rows == first
        rk = jnp.where(pick, float(r), rk)
        s = jnp.where(pick, -jnp.inf, s)
        vals = jnp.where(vrow == r, mx, vals)
    return vals, rk


def _num_ranked(rk):
    return jnp.sum(jnp.where(rk < float(PEER_TOPK), 1.0, 0.0), axis=0, keepdims=True)


def _gather_rows(p, v):
    return jnp.dot(p, v, preferred_element_type=f32, precision=lax.Precision.HIGHEST)


def _peer_query_body(x_ref, sh_ref, sc_ref, wq_ref, o_ref):
    hmod = _modulate(x_ref[...], sh_ref[...], sc_ref[...]).astype(bf16)
    nk = PEER_HALF
    for c in range(0, wq_ref.shape[1], 4 * nk):
        r = jnp.dot(hmod, wq_ref[:, c:c + 4 * nk], preferred_element_type=f32)
        for k in range(4):
            o_ref[c // nk + k] = r[:, k * nk:(k + 1) * nk].astype(bf16)


def _peer_query(x, shift, scale, wq, tm, name):
    m, d = x.shape
    nset = 2 * PEER_HEADS
    row_of = lambda i: i
    return pl.pallas_call(
        _peer_query_body,
        grid=(m // tm,),
        in_specs=[pl.BlockSpec((tm, d), lambda i: (i, 0)), shift.spec(row_of), scale.spec(row_of),
                  pl.BlockSpec(wq.shape, lambda i: (0, 0))],
        out_specs=pl.BlockSpec((nset, tm, PEER_HALF), lambda i: (0, i, 0)),
        out_shape=jax.ShapeDtypeStruct((nset, m, PEER_HALF), bf16),
        compiler_params=_params(("arbitrary",), 32),
        name=name,
    )(x, shift.arr, scale.arr, wq)


def _pack_rows(x):
    return pltpu.bitcast(x.astype(bf16), i32)


def _unpack_rows(w):
    return pltpu.bitcast(w, bf16)


def _dup_words(x):
    hi = lax.bitcast_convert_type(x.astype(bf16).astype(f32), jnp.uint32)
    return lax.bitcast_convert_type(hi | (hi >> 16), i32)


def _peer_route_body(q_ref, keys_ref, p0_ref, p1_ref, p0t_ref, r1_ref, e1_ref, c0_ref, ez_ref,
                     v0_scr, rk0_scr, v1_scr, rk1_scr):
    G = V7X_SUBLANES
    ngrp = PEER_NCELL_PAD // G
    sub = lax.broadcasted_iota(i32, (G, q_ref.shape[1]), 0)

    def head(h, carry):
        s0 = lax.dot_general(keys_ref[2 * h], q_ref[2 * h], NT_DIMS, preferred_element_type=f32)
        s1 = lax.dot_general(keys_ref[2 * h + 1], q_ref[2 * h + 1], NT_DIMS, preferred_element_type=f32)
        v0_scr[...], rk0_scr[...] = _top16(s0, break_ties=False)
        v1_scr[...], rk1_scr[...] = _top16(s1, break_ties=False)
        extra = (jnp.abs(_num_ranked(rk0_scr[...]) - float(PEER_TOPK))
                 + jnp.abs(_num_ranked(rk1_scr[...]) - float(PEER_TOPK)))

        @pl.when(jnp.max(extra) > 0.0)
        def _():
            v0_scr[...], rk0_scr[...] = _top16(s0, break_ties=True)
            v1_scr[...], rk1_scr[...] = _top16(s1, break_ties=True)

        v0, rk0 = v0_scr[...], rk0_scr[...]
        v1, rk1 = v1_scr[...], rk1_scr[...]
        cand = _gather_rows(p0_ref[...], v0) + _gather_rows(p1_ref[...], v1)
        groups = [cand[g * G:(g + 1) * G] for g in range(ngrp)]
        counts = [jnp.zeros(groups[0].shape, f32) for _ in range(ngrp)]
        for y in range(PEER_NCELL):
            gy, ry = divmod(y, G)
            cy = groups[gy][ry:ry + 1]
            for g in range(ngrp):
                gt = jnp.where(cy > groups[g], 1.0, 0.0)
                ge = jnp.where(cy >= groups[g], 1.0, 0.0)
                if g < gy:
                    ahead = gt
                elif g > gy:
                    ahead = ge
                else:
                    ahead = jnp.where(sub > ry, ge, gt)
                counts[g] = counts[g] + ahead
        cell = lax.broadcasted_iota(i32, cand.shape, 0)
        selm = jnp.where(cell < PEER_NCELL,
                         jnp.where(jnp.concatenate(counts, axis=0) < float(PEER_TOPK), 1.0, 0.0), 0.0)
        cnt = jnp.dot(p0t_ref[...], selm, preferred_element_type=f32)
        ev0 = jnp.exp(v0 - v0[0:1])
        ev1 = jnp.exp(v1 - v1[0:1])
        z = jnp.sum(selm * _gather_rows(p0_ref[...], ev0) * _gather_rows(p1_ref[...], ev1),
                    axis=0, keepdims=True)
        c0 = jnp.zeros(s0.shape, f32)
        for a in range(PEER_TOPK):
            c0 = jnp.where(rk0 == float(a), cnt[a:a + 1], c0)
        r1_ref[h] = _pack_rows(rk1)
        e1_ref[h] = _pack_rows(jnp.exp(s1 - v1[0:1]))
        c0_ref[h] = _dup_words(c0)
        ez_ref[h] = _dup_words(0.5 * jnp.exp(s0 - v0[0:1]) / z)
        return carry

    lax.fori_loop(0, PEER_HEADS, head, 0)


def _peer_route(q, keys, tm, name):
    nset, m, half = q.shape
    p0, p1, p0t = _peer_cell_tables()
    full = lambda a: pl.BlockSpec(a.shape, lambda i: (0,) * a.ndim)
    pair = pl.BlockSpec((PEER_HEADS, PEER_NKEYS // 2, tm), lambda i: (0, 0, i))
    dup = pl.BlockSpec((PEER_HEADS, PEER_NKEYS, tm), lambda i: (0, 0, i))
    return pl.pallas_call(
        _peer_route_body,
        grid=(m // tm,),
        in_specs=[pl.BlockSpec((nset, tm, half), lambda i: (0, i, 0)), full(keys), full(p0), full(p1), full(p0t)],
        out_specs=[pair, pair, dup, dup],
        out_shape=[jax.ShapeDtypeStruct((PEER_HEADS, PEER_NKEYS // 2, m), i32)] * 2
        + [jax.ShapeDtypeStruct((PEER_HEADS, PEER_NKEYS, m), i32)] * 2,
        scratch_shapes=[pltpu.VMEM((PEER_TOPK, tm), f32), pltpu.VMEM((PEER_NKEYS, tm), f32)] * 2,
        compiler_params=_params(("arbitrary",), 40),
        name=name,
    )(q, keys, p0, p1, p0t)


PEER_TE = V7X_SUBLANES * PEER_NKEYS
PEER_TOKEN_CHUNK = 256


def _peer_expert_body(x_ref, sh_ref, sc_ref, gate_ref, g_ref, b_ref, u_ref, vt_ref,
                      r1_ref, e1_ref, c0_ref, ez_ref, o_ref, h_scr, acc_scr, hs_scr):
    e = pl.program_id(1)
    tn = x_ref.shape[0]
    nk = PEER_NKEYS
    tc = min(tn, PEER_TOKEN_CHUNK)

    @pl.when(e == 0)
    def _():
        h_scr[...] = _modulate(x_ref[...], sh_ref[...], sc_ref[...]).astype(bf16)
        acc_scr[...] = jnp.zeros(acc_scr.shape, f32)

    igrp = pl.ds(pl.multiple_of(e * V7X_SUBLANES, V7X_SUBLANES), V7X_SUBLANES)

    def gates(c):
        for il in range(PEER_TE // nk):
            for lg in range(tc // V7X_LANES):
                ln = slice(c * tc + lg * V7X_LANES, c * tc + (lg + 1) * V7X_LANES)
                terms = []
                for hh in range(PEER_HEADS):
                    crow = _unpack_rows(jnp.broadcast_to(c0_ref[hh, igrp, ln][il:il + 1], (nk // 2, V7X_LANES)))
                    zrow = _unpack_rows(jnp.broadcast_to(ez_ref[hh, igrp, ln][il:il + 1], (nk // 2, V7X_LANES)))
                    r1 = _unpack_rows(r1_ref[hh, :, ln])
                    e1 = _unpack_rows(e1_ref[hh, :, ln])
                    terms.append(jnp.where(r1 < crow, e1 * zrow, jnp.zeros((), bf16)))
                while len(terms) > 1:
                    terms = [terms[k] + terms[k + 1] for k in range(0, len(terms), 2)]
                hs_scr[il * nk:(il + 1) * nk, ln] = terms[0]

    def activate(c, pre):
        for il in range(PEER_TE // nk):
            for lg in range(tc // V7X_LANES):
                ln = slice(c * tc + lg * V7X_LANES, c * tc + (lg + 1) * V7X_LANES)
                rows = slice(il * nk, (il + 1) * nk)
                a = pre[rows, lg * V7X_LANES:(lg + 1) * V7X_LANES]
                act = a * (1.0 + lax.erf(a * INV_SQRT2))
                hs_scr[rows, ln] = hs_scr[rows, ln] * act.astype(bf16)
        cs = slice(c * tc, (c + 1) * tc)
        acc_scr[:, cs] += jnp.dot(vt_ref[...], hs_scr[:, cs], preferred_element_type=f32)

    nchunk = tn // tc
    gates(0)
    pre = [lax.dot_general(u_ref[...], h_scr[c * tc:(c + 1) * tc, :], NT_DIMS, preferred_element_type=f32)
           for c in range(nchunk)]
    for c in range(nchunk):
        if c + 1 < nchunk:
            gates(c + 1)
        activate(c, pre[c])

    @pl.when(e == pl.num_programs(1) - 1)
    def _():
        f = acc_scr[...].T
        y = DN_ALPHA * x_ref[...] + gate_ref[...] * f
        o_ref[...] = _layer_norm(y, g_ref[...], b_ref[...])


def _peer_experts(x, shift, scale, gate, ln_g, ln_b, u, vt, route, tm, name):
    m, d = x.shape
    ne = u.shape[0] // PEER_TE
    row_of = lambda i, e: i
    vec = pl.BlockSpec((1, d), lambda i, e: (0, 0))
    pair = pl.BlockSpec((PEER_HEADS, PEER_NKEYS // 2, tm), lambda i, e: (0, 0, i))
    dup = pl.BlockSpec((PEER_HEADS, PEER_NKEYS, tm), lambda i, e: (0, 0, i))
    return pl.pallas_call(
        _peer_expert_body,
        grid=(m // tm, ne),
        in_specs=[pl.BlockSpec((tm, d), lambda i, e: (i, 0)),
                  shift.spec(row_of), scale.spec(row_of), gate.spec(row_of), vec, vec,
                  pl.BlockSpec((PEER_TE, d), lambda i, e: (e, 0)),
                  pl.BlockSpec((None, d, PEER_TE), lambda i, e: (e, 0, 0)),
                  pair, pair, dup, dup],
        out_specs=pl.BlockSpec((tm, d), lambda i, e: (i, 0)),
        out_shape=jax.ShapeDtypeStruct((m, d), f32),
        scratch_shapes=[pltpu.VMEM((tm, d), bf16), pltpu.VMEM((d, tm), f32), pltpu.VMEM((PEER_TE, tm), bf16)],
        compiler_params=_params(("arbitrary", "arbitrary"), 48),
        name=name,
    )(x, shift.arr, scale.arr, gate.arr, ln_g.reshape(1, d), ln_b.reshape(1, d), u, vt, *route)


def kernel(x_prompt, x_sample, cache_k, cache_v, state_ret, page_table, c_prompt, c_sample, w_ada, b_ada,
           ln_g, ln_b, w_qkv_attn, w_o_attn, w_in_ret, w_o_ret, w_q_peer, keys_peer, u_peer, v_peer):
    bp, tp, d = x_prompt.shape
    bs, ts, _ = x_sample.shape
    n_p, n_s = bp * tp, bs * ts
    tm_p = 512
    depth = w_ada.shape[0]

    c_all = jnp.concatenate([c_prompt, c_sample], axis=0)
    pad = (-c_all.shape[0]) % V7X_SUBLANES
    ada = _ada(jnp.pad(c_all, ((0, pad), (0, 0))), w_ada, b_ada)

    slopes = 2.0 ** (-8.0 * jnp.arange(1, N_HEADS + 1, dtype=f32) / N_HEADS)
    ck_t = cache_k.transpose(0, 1, 3, 4, 2)
    cv_t = cache_v.transpose(0, 1, 3, 4, 2)

    yp = x_prompt.reshape(n_p, d)
    ys = x_sample.reshape(n_s, d)
    kp_l, vp_l, ks_l, vs_l, sp_l, ss_l = [], [], [], [], [], []
    for i in range(depth):
        mods_p = [_Mod(ada[i, :bp, c * d:(c + 1) * d], tm_p, tp) for c in range(6)]
        mods_s = [_Mod(ada[i, bp:bp + bs, c * d:(c + 1) * d], n_s, ts) for c in range(6)]
        j = i // 2
        if i % 2 == 0:
            w = w_qkv_attn[j].astype(bf16)
            wo = w_o_attn[j].astype(bf16)
            qh, kh, vt_h, kp_t, vp_t, kmean = _qkv_heads(yp, mods_p[0], mods_p[1], w, bp, tp, tm_p, "qkv_prompt")
            ap = _moba_prompt(qh, kh, vt_h, kmean, slopes)
            qs, _, kss, _, vss, _ = _qkv(ys, mods_s[0], mods_s[1], w, n_s, "qkv_sample")
            to_t = lambda a: a.reshape(bs, ts, N_HEADS, HEAD_DIM).transpose(0, 2, 3, 1)
            q_t = to_t(qs)
            scores = _sample_block_scores(ck_t, j, page_table, q_t)
            sel = _sample_topk(scores.reshape(bs, scores.shape[1], N_HEADS * ts))
            sel_flat = sel.reshape(bs, MOBA_TOPK, N_HEADS, ts).transpose(0, 2, 3, 1).reshape(-1)
            a_t = _sample_attention(sel_flat, page_table, slopes, q_t, to_t(kss), to_t(vss), ck_t, cv_t, j)
            a_s = a_t.transpose(0, 3, 1, 2).reshape(n_s, d)
            kp_l.append(kp_t.transpose(0, 3, 1, 2))
            vp_l.append(vp_t.transpose(0, 3, 1, 2))
            ks_l.append(kss.reshape(bs, ts, N_HEADS, HEAD_DIM))
            vs_l.append(vss.reshape(bs, ts, N_HEADS, HEAD_DIM))
        else:
            w = w_in_ret[j].astype(bf16)
            wo = w_o_ret[j].astype(bf16)
            zp = _mod_matmul(yp, mods_p[0], mods_p[1], w, tm_p, 2048, "ret_in_prompt")
            ap, s_fin = _retention_prompt(zp, bp, tp)
            zs = _mod_matmul(ys, mods_s[0], mods_s[1], w, n_s, 2048, "ret_in_sample")
            a_s, s_new = _retention_sample(zs, state_ret[j], bs, ts)
            sp_l.append(s_fin)
            ss_l.append(s_new)
        yp = _proj_res_ln(ap, wo, yp, mods_p[2], ln_g[i, 0], ln_b[i, 0], tm_p, "mixer_out_prompt")
        ys = _proj_res_ln(a_s, wo, ys, mods_s[2], ln_g[i, 0], ln_b[i, 0], n_s, "mixer_out_sample")

        wq = w_q_peer[i].astype(bf16)
        keys = keys_peer[i].reshape(2 * PEER_HEADS, PEER_NKEYS, PEER_HALF).astype(bf16)
        u = u_peer[i].astype(bf16)
        vt = v_peer[i].astype(bf16).reshape(PEER_EXPERTS // PEER_TE, PEER_TE, d).transpose(0, 2, 1)
        route_p = _peer_route(_peer_query(yp, mods_p[3], mods_p[4], wq, tm_p, "peer_query_prompt"),
                              keys, 2 * V7X_LANES, "peer_route_prompt")
        yp = _peer_experts(yp, mods_p[3], mods_p[4], mods_p[5], ln_g[i, 1], ln_b[i, 1], u, vt, route_p,
                           tm_p, "peer_experts_prompt")
        route_s = _peer_route(_peer_query(ys, mods_s[3], mods_s[4], wq, n_s, "peer_query_sample"),
                              keys, n_s, "peer_route_sample")
        ys = _peer_experts(ys, mods_s[3], mods_s[4], mods_s[5], ln_g[i, 1], ln_b[i, 1], u, vt, route_s,
                           n_s, "peer_experts_sample")

    return (yp.reshape(bp, tp, d), ys.reshape(bs, ts, d),
            jnp.stack(kp_l), jnp.stack(vp_l), jnp.stack(ks_l), jnp.stack(vs_l),
            jnp.stack(sp_l), jnp.stack(ss_l))
```

```python
import functools

import jax
import jax.numpy as jnp
import numpy as np
from jax import lax
from jax.experimental import pallas as pl
from jax.experimental.pallas import tpu as pltpu

f32 = jnp.float32
bf16 = jnp.bfloat16
i32 = jnp.int32

D_MODEL = 1024
DEPTH = 2
PAGE_SIZE = 128
N_HEADS = 16
HEAD_DIM = D_MODEL // N_HEADS
MOBA_BLOCK = 256
MOBA_TOPK = 3
R_HEADS = 4
R_DK = D_MODEL // R_HEADS
R_DV = 2 * R_DK
R_Q = R_HEADS * R_DK
R_V = R_HEADS * R_DV
R_IN = 2 * R_Q + 2 * R_V
R_CHUNK = 128
PEER_HEADS = 8
PEER_NKEYS = 128
PEER_EXPERTS = PEER_NKEYS * PEER_NKEYS
PEER_HALF = 128
PEER_TOPK = 16
DN_ALPHA = (2 * DEPTH) ** 0.25
LN_EPS = 1e-5

V7X_LANES = 128
V7X_SUBLANES = 8
V7X_VMEM_BYTES = 64 * 1024 * 1024

NEG = -1e30
INV_SQRT2 = 0.7071067811865476

NT_DIMS = (((1,), (1,)), ((), ()))
TN_DIMS = (((0,), (0,)), ((), ()))


def _vmem(mib):
    assert mib * 1024 * 1024 < V7X_VMEM_BYTES
    return mib * 1024 * 1024


def _params(sem, mib):
    return pltpu.CompilerParams(dimension_semantics=sem, vmem_limit_bytes=_vmem(mib))


def _modulate(x, shift, scale):
    return x * (1.0 + scale) + shift


def _layer_norm(y, g, b):
    mu = jnp.mean(y, axis=-1, keepdims=True)
    yc = y - mu
    var = jnp.mean(yc * yc, axis=-1, keepdims=True)
    return yc * lax.rsqrt(var + LN_EPS) * g + b


class _Mod:
    def __init__(self, vec, tm, rows_per_group):
        self.tm = tm
        if rows_per_group % tm == 0:
            self.arr = vec[:, None, :]
            self.tiles_per_group = rows_per_group // tm
            self.per_row = False
        else:
            self.arr = jnp.repeat(vec, rows_per_group, axis=0)
            self.per_row = True

    def spec(self, row_of):
        d = self.arr.shape[-1]
        if self.per_row:
            return pl.BlockSpec((self.tm, d), lambda *g: (row_of(*g), 0))
        tpg = self.tiles_per_group
        return pl.BlockSpec((None, 1, d), lambda *g: (row_of(*g) // tpg, 0, 0))


def _ada_body(c_ref, w_ref, b_ref, o_ref):
    c = c_ref[...]
    a = c * (1.0 / (1.0 + jnp.exp(-c)))
    o_ref[...] = jnp.dot(a, w_ref[...], preferred_element_type=f32,
                         precision=lax.Precision.HIGHEST) + b_ref[...]


def _ada(c_all, w_ada, b_ada):
    depth, d, n6 = w_ada.shape
    rows = c_all.shape[0]
    tn = 1024
    return pl.pallas_call(
        _ada_body,
        grid=(depth, n6 // tn),
        in_specs=[pl.BlockSpec((rows, d), lambda l, j: (0, 0)),
                  pl.BlockSpec((None, d, tn), lambda l, j: (l, 0, j)),
                  pl.BlockSpec((None, 1, tn), lambda l, j: (l, 0, j))],
        out_specs=pl.BlockSpec((None, rows, tn), lambda l, j: (l, 0, j)),
        out_shape=jax.ShapeDtypeStruct((depth, rows, n6), f32),
        compiler_params=_params(("arbitrary", "arbitrary"), 24),
        name="ada",
    )(c_all, w_ada, b_ada.reshape(depth, 1, n6))


def _mod_mm_body(x_ref, sh_ref, sc_ref, w_ref, o_ref):
    h = _modulate(x_ref[...], sh_ref[...], sc_ref[...]).astype(bf16)
    n = o_ref.shape[1]
    for c in range(0, n, 512):
        o_ref[:, c:c + 512] = jnp.dot(h, w_ref[:, c:c + 512], preferred_element_type=f32)


def _mod_matmul(x, shift, scale, w, tm, tn, name):
    m, d = x.shape
    n = w.shape[1]
    row_of = lambda j, i: i
    return pl.pallas_call(
        _mod_mm_body,
        grid=(n // tn, m // tm),
        in_specs=[pl.BlockSpec((tm, d), lambda j, i: (i, 0)),
                  shift.spec(row_of), scale.spec(row_of),
                  pl.BlockSpec((d, tn), lambda j, i: (0, j))],
        out_specs=pl.BlockSpec((tm, tn), lambda j, i: (i, j)),
        out_shape=jax.ShapeDtypeStruct((m, n), f32),
        compiler_params=_params(("arbitrary", "arbitrary"), 40),
        name=name,
    )(x, shift.arr, scale.arr, w)


def _qkv_body(x_ref, sh_ref, sc_ref, w_ref, q_ref, qb_ref, k_ref, kb_ref, v_ref, vb_ref):
    h = _modulate(x_ref[...], sh_ref[...], sc_ref[...]).astype(bf16)
    d = D_MODEL
    for c in range(0, d, 512):
        q = jnp.dot(h, w_ref[:, c:c + 512], preferred_element_type=f32)
        q_ref[:, c:c + 512] = q
        qb_ref[:, c:c + 512] = (q * (HEAD_DIM ** -0.5)).astype(bf16)
        k = jnp.dot(h, w_ref[:, d + c:d + c + 512], preferred_element_type=f32)
        k_ref[:, c:c + 512] = k
        kb_ref[:, c:c + 512] = k.astype(bf16)
        v = jnp.dot(h, w_ref[:, 2 * d + c:2 * d + c + 512], preferred_element_type=f32)
        v_ref[:, c:c + 512] = v
        vb_ref[:, c:c + 512] = v.astype(bf16)


def _qkv(x, shift, scale, w, tm, name):
    m, d = x.shape
    row_of = lambda i: i
    blk = pl.BlockSpec((tm, d), lambda i: (i, 0))
    return pl.pallas_call(
        _qkv_body,
        grid=(m // tm,),
        in_specs=[blk, shift.spec(row_of), scale.spec(row_of),
                  pl.BlockSpec((d, 3 * d), lambda i: (0, 0))],
        out_specs=[blk] * 6,
        out_shape=[jax.ShapeDtypeStruct((m, d), t) for t in (f32, bf16, f32, bf16, f32, bf16)],
        compiler_params=_params(("arbitrary",), 48),
        name=name,
    )(x, shift.arr, scale.arr, w)


def _qkv_heads_body(x_ref, sh_ref, sc_ref, w_ref, qh_ref, kh_ref, vt_ref, kt_ref, vtf_ref, km_ref):
    h = _modulate(x_ref[...], sh_ref[...], sc_ref[...]).astype(bf16)
    d, dh, L = D_MODEL, HEAD_DIM, MOBA_BLOCK
    tm = x_ref.shape[0]
    cw = 512
    for c in range(0, d, cw):
        q = jnp.dot(h, w_ref[:, c:c + cw], preferred_element_type=f32) * (dh ** -0.5)
        k = jnp.dot(h, w_ref[:, d + c:d + c + cw], preferred_element_type=f32)
        v = jnp.dot(h, w_ref[:, 2 * d + c:2 * d + c + cw], preferred_element_type=f32)
        k_t = k.T
        v_t = v.T
        for blk in range(tm // L):
            km_ref[blk, :, c:c + cw] = jnp.sum(k[blk * L:(blk + 1) * L], axis=0, keepdims=True) * (1.0 / L)
        for hh in range(cw // dh):
            head = c // dh + hh
            cols = slice(hh * dh, (hh + 1) * dh)
            qh_ref[head] = q[:, cols].astype(bf16)
            kt_ref[head] = k_t[cols, :]
            vtf_ref[head] = v_t[cols, :]
            for blk in range(tm // L):
                kh_ref[head, blk] = k[blk * L:(blk + 1) * L, cols].astype(bf16)
                vt_ref[head, blk] = v_t[cols, blk * L:(blk + 1) * L].astype(bf16)


def _qkv_heads(x, shift, scale, w, b, t, tm, name):
    m, d = x.shape
    nh, dh, L = N_HEADS, HEAD_DIM, MOBA_BLOCK
    nb, tpb, bpt = t // L, t // tm, tm // L
    row_of = lambda bi, i: bi * tpb + i
    return pl.pallas_call(
        _qkv_heads_body,
        grid=(b, tpb),
        in_specs=[pl.BlockSpec((tm, d), lambda bi, i: (bi * tpb + i, 0)),
                  shift.spec(row_of), scale.spec(row_of),
                  pl.BlockSpec((d, 3 * d), lambda bi, i: (0, 0))],
        out_specs=[pl.BlockSpec((None, nh, tm, dh), lambda bi, i: (bi, 0, i, 0)),
                   pl.BlockSpec((None, nh, bpt, L, dh), lambda bi, i: (bi, 0, i, 0, 0)),
                   pl.BlockSpec((None, nh, bpt, dh, L), lambda bi, i: (bi, 0, i, 0, 0)),
                   pl.BlockSpec((None, nh, dh, tm), lambda bi, i: (bi, 0, 0, i)),
                   pl.BlockSpec((None, nh, dh, tm), lambda bi, i: (bi, 0, 0, i)),
                   pl.BlockSpec((None, bpt, 1, d), lambda bi, i: (bi, i, 0, 0))],
        out_shape=[jax.ShapeDtypeStruct((b, nh, t, dh), bf16),
                   jax.ShapeDtypeStruct((b, nh, nb, L, dh), bf16),
                   jax.ShapeDtypeStruct((b, nh, nb, dh, L), bf16),
                   jax.ShapeDtypeStruct((b, nh, dh, t), f32),
                   jax.ShapeDtypeStruct((b, nh, dh, t), f32),
                   jax.ShapeDtypeStruct((b, nb, 1, d), f32)],
        compiler_params=_params(("arbitrary", "arbitrary"), 56),
        name=name,
    )(x, shift.arr, scale.arr, w)


def _proj_ln_body(a_ref, w_ref, x_ref, gate_ref, g_ref, b_ref, o_ref):
    f = jnp.dot(a_ref[...].astype(bf16), w_ref[...], preferred_element_type=f32)
    y = DN_ALPHA * x_ref[...] + gate_ref[...] * f
    o_ref[...] = _layer_norm(y, g_ref[...], b_ref[...])


def _proj_res_ln(a, w, x, gate, ln_g, ln_b, tm, name):
    m, k = a.shape
    d = x.shape[1]
    row_of = lambda i: i
    vec = pl.BlockSpec((1, d), lambda i: (0, 0))
    return pl.pallas_call(
        _proj_ln_body,
        grid=(m // tm,),
        in_specs=[pl.BlockSpec((tm, k), lambda i: (i, 0)),
                  pl.BlockSpec((k, d), lambda i: (0, 0)),
                  pl.BlockSpec((tm, d), lambda i: (i, 0)),
                  gate.spec(row_of), vec, vec],
        out_specs=pl.BlockSpec((tm, d), lambda i: (i, 0)),
        out_shape=jax.ShapeDtypeStruct((m, d), f32),
        compiler_params=_params(("arbitrary",), 40),
        name=name,
    )(a, w, x, gate.arr, ln_g.reshape(1, d), ln_b.reshape(1, d))


def _kmean_body(k_ref, o_ref):
    o_ref[...] = jnp.sum(k_ref[...], axis=0, keepdims=True) * (1.0 / MOBA_BLOCK)


def _block_means(k):
    m, d = k.shape
    nb = m // MOBA_BLOCK
    return pl.pallas_call(
        _kmean_body,
        grid=(nb,),
        in_specs=[pl.BlockSpec((MOBA_BLOCK, d), lambda i: (i, 0))],
        out_specs=pl.BlockSpec((None, 1, d), lambda i: (i, 0, 0)),
        out_shape=jax.ShapeDtypeStruct((nb, 1, d), f32),
        compiler_params=_params(("arbitrary",), 16),
        name="moba_block_means",
    )(k)


MOBA_HEADS_PER_STEP = 8


def _moba_body(sl_ref, q_ref, k_ref, vt_ref, km_ref, o_ref, sel_ref, base_ref):
    hg = q_ref.shape[0]
    dh = q_ref.shape[2]
    h0 = pl.program_id(1) * hg
    qt = pl.program_id(2)
    L = MOBA_BLOCK
    nb = km_ref.shape[0]
    ki = lax.broadcasted_iota(i32, (L, L), 0)
    qi = lax.broadcasted_iota(i32, (L, L), 1)

    @pl.when(qt == 0)
    def _():
        rel = (qi - ki).astype(f32)
        for hh in range(hg):
            base_ref[hh] = (-sl_ref[h0 + hh]) * rel

    state = []
    for hh in range(hg):
        q = q_ref[hh]
        km = km_ref[:, 0, hh * dh:(hh + 1) * dh]
        sc = lax.dot_general(km.astype(bf16), q, NT_DIMS, preferred_element_type=f32)
        rows = lax.broadcasted_iota(i32, sc.shape, 0)
        sc = jnp.where(rows < qt, sc, -jnp.inf)
        sel = jnp.zeros(sc.shape, f32)
        for r in range(MOBA_TOPK):
            mx = jnp.max(sc, axis=0, keepdims=True)
            first = jnp.min(jnp.where(sc == mx, rows, nb), axis=0, keepdims=True)
            pick = rows == first
            slot_ok = jnp.where(qt > r, 1.0, 0.0)
            sel = jnp.maximum(sel, jnp.where(pick, slot_ok, 0.0))
            sc = jnp.where(pick, -jnp.inf, sc)
        sel_ref[hh] = sel
        s = lax.dot_general(k_ref[hh, qt], q, NT_DIMS, preferred_element_type=f32) + base_ref[hh]
        s = jnp.where(ki <= qi, s, NEG)
        m0 = jnp.max(s, axis=0, keepdims=True)
        p = jnp.exp(s - m0)
        l0 = jnp.sum(p, axis=0, keepdims=True)
        state += [m0, l0, jnp.dot(vt_ref[hh, qt], p.astype(bf16), preferred_element_type=f32)]

    def step(mb, carry):
        scores = [lax.dot_general(k_ref[hh, mb], q_ref[hh], NT_DIMS, preferred_element_type=f32)
                  for hh in range(hg)]
        out, pv = [], []
        for hh in range(hg):
            m, l, acc = carry[3 * hh:3 * hh + 3]
            cm = (qt - mb).astype(f32) * ((-sl_ref[h0 + hh]) * L)
            addrow = jnp.where(sel_ref[hh, pl.ds(mb, 1), :] > 0.0, cm, NEG)
            s = scores[hh] + base_ref[hh] + addrow
            m_new = jnp.maximum(m, jnp.max(s, axis=0, keepdims=True))
            alpha = jnp.exp(m - m_new)
            p = jnp.exp(s - m_new)
            l = l * alpha + jnp.sum(p, axis=0, keepdims=True)
            out += [m_new, l, acc * alpha]
            pv.append(p.astype(bf16))
        for hh in range(hg):
            out[3 * hh + 2] = out[3 * hh + 2] + jnp.dot(vt_ref[hh, mb], pv[hh], preferred_element_type=f32)
        return tuple(out)

    final = lax.fori_loop(0, qt, step, tuple(state))
    out_t = jnp.concatenate([final[3 * hh + 2] / final[3 * hh + 1] for hh in range(hg)], axis=0)
    o_ref[...] = out_t.T


def _moba_prompt(qh, kh, vt, km, slopes):
    b, nh, t, dh = qh.shape
    L, hg = MOBA_BLOCK, MOBA_HEADS_PER_STEP
    nb = t // L
    return pl.pallas_call(
        _moba_body,
        grid_spec=pltpu.PrefetchScalarGridSpec(
            num_scalar_prefetch=1,
            grid=(b, nh // hg, nb),
            in_specs=[pl.BlockSpec((None, hg, L, dh), lambda bi, g, i, sl: (bi, g, i, 0)),
                      pl.BlockSpec((None, hg, nb, L, dh), lambda bi, g, i, sl: (bi, g, 0, 0, 0)),
                      pl.BlockSpec((None, hg, nb, dh, L), lambda bi, g, i, sl: (bi, g, 0, 0, 0)),
                      pl.BlockSpec((None, nb, 1, hg * dh), lambda bi, g, i, sl: (bi, 0, 0, g))],
            out_specs=pl.BlockSpec((L, hg * dh), lambda bi, g, i, sl: (bi * nb + i, g)),
            scratch_shapes=[pltpu.VMEM((hg, nb, L), f32), pltpu.VMEM((hg, L, L), f32)]),
        out_shape=jax.ShapeDtypeStruct((b * t, nh * dh), f32),
        compiler_params=_params(("arbitrary", "arbitrary", "arbitrary"), 40),
        name="moba_prompt_attn",
    )(slopes, qh, kh, vt, km)


SAMPLE_PAGES_PER_STEP = 16


def _sample_scores_body(pt_ref, *refs):
    npg = SAMPLE_PAGES_PER_STEP
    k_refs, q_ref, o_ref = refs[:npg], refs[npg], refs[npg + 1]
    j = pl.program_id(1)
    ppb = MOBA_BLOCK // PAGE_SIZE
    for bb in range(npg // ppb):
        tot = k_refs[bb * ppb][...]
        for pg in range(1, ppb):
            tot = tot + k_refs[bb * ppb + pg][...]
        kmean = jnp.sum(tot, axis=-1, keepdims=True) * (1.0 / MOBA_BLOCK)
        o_ref[j * (npg // ppb) + bb] = jnp.sum(kmean * q_ref[...], axis=1)


def _sample_block_scores(ck_t, layer, page_table, q_t):
    db, n_pages = page_table.shape
    _, _, nh, dh, pg = ck_t.shape
    tn = q_t.shape[-1]
    npg = SAMPLE_PAGES_PER_STEP
    nbp = n_pages * PAGE_SIZE // MOBA_BLOCK

    def page_spec(o):
        return pl.BlockSpec((None, None, nh, dh, pg), lambda s, j, pt: (layer, pt[s, j * npg + o], 0, 0, 0))

    return pl.pallas_call(
        _sample_scores_body,
        grid_spec=pltpu.PrefetchScalarGridSpec(
            num_scalar_prefetch=1,
            grid=(db, n_pages // npg),
            in_specs=[page_spec(o) for o in range(npg)]
            + [pl.BlockSpec((None, nh, dh, tn), lambda s, j, pt: (s, 0, 0, 0))],
            out_specs=pl.BlockSpec((None, nbp, nh, tn), lambda s, j, pt: (s, 0, 0, 0))),
        out_shape=jax.ShapeDtypeStruct((db, nbp, nh, tn), f32),
        compiler_params=_params(("arbitrary", "arbitrary"), 32),
        name="moba_sample_block_scores",
    )(page_table, *([ck_t] * npg), q_t)


def _sample_topk_body(s_ref, o_ref):
    s = s_ref[...]
    nbp = s.shape[0]
    rows = lax.broadcasted_iota(i32, s.shape, 0)
    for r in range(MOBA_TOPK):
        mx = jnp.max(s, axis=0, keepdims=True)
        first = jnp.min(jnp.where(s == mx, rows, nbp), axis=0, keepdims=True)
        o_ref[r:r + 1, :] = jnp.minimum(first, nbp - 1)
        s = jnp.where(rows == first, -jnp.inf, s)


def _sample_topk(scores):
    db, nbp, c = scores.shape
    return pl.pallas_call(
        _sample_topk_body,
        grid=(db,),
        in_specs=[pl.BlockSpec((None, nbp, c), lambda s: (s, 0, 0))],
        out_specs=pl.BlockSpec((None, MOBA_TOPK, c), lambda s: (s, 0, 0)),
        out_shape=jax.ShapeDtypeStruct((db, MOBA_TOPK, c), i32),
        compiler_params=_params(("arbitrary",), 16),
        name="moba_sample_topk",
    )(scores)


def _sample_page_copies(sel_ref, pt_ref, ck_ref, cv_ref, kbuf, vbuf, sem, step, slot, layer, tn):
    ppb = MOBA_BLOCK // PAGE_SIZE
    s = step // N_HEADS
    h = step % N_HEADS
    out = []
    for qi in range(tn):
        for sl in range(MOBA_TOPK):
            blk = sel_ref[(step * tn + qi) * MOBA_TOPK + sl]
            for pg in range(ppb):
                page = pt_ref[s, blk * ppb + pg]
                idx = (qi * MOBA_TOPK + sl) * ppb + pg
                out.append(pltpu.make_async_copy(ck_ref.at[layer, page, h], kbuf.at[slot, idx], sem.at[slot]))
                out.append(pltpu.make_async_copy(cv_ref.at[layer, page, h], vbuf.at[slot, idx], sem.at[slot]))
    return out


def _sample_attn_body(sel_ref, pt_ref, sl_ref, q_ref, kn_ref, vn_ref, ck_ref, cv_ref, o_ref,
                      kbuf, vbuf, sem, *, layer, past_len):
    tn = q_ref.shape[1]
    ppb = MOBA_BLOCK // PAGE_SIZE
    n = pl.program_id(0)
    nsteps = pl.num_programs(0)
    slot = n % 2
    args = (sel_ref, pt_ref, ck_ref, cv_ref, kbuf, vbuf, sem)

    @pl.when(n == 0)
    def _():
        for c in _sample_page_copies(*args, 0, 0, layer, tn):
            c.start()

    @pl.when(n + 1 < nsteps)
    def _():
        for c in _sample_page_copies(*args, n + 1, 1 - slot, layer, tn):
            c.start()

    for c in _sample_page_copies(*args, n, slot, layer, tn):
        c.wait()

    h = n % N_HEADS
    slope = sl_ref[h]
    lane = lax.broadcasted_iota(i32, (1, PAGE_SIZE), 1).astype(f32)
    jn = lax.broadcasted_iota(i32, (1, tn), 1)
    kn = kn_ref[...]
    vn = vn_ref[...]
    for qi in range(tn):
        qcol = q_ref[:, qi:qi + 1] * (HEAD_DIM ** -0.5)
        t_q = past_len + qi
        s_rows = []
        for sl in range(MOBA_TOPK):
            blk = sel_ref[(n * tn + qi) * MOBA_TOPK + sl]
            for pg in range(ppb):
                idx = (qi * MOBA_TOPK + sl) * ppb + pg
                sc = jnp.sum(kbuf[slot, idx] * qcol, axis=0, keepdims=True)
                dist0 = (t_q - (blk * MOBA_BLOCK + pg * PAGE_SIZE)).astype(f32)
                s_rows.append(sc - slope * (dist0 - lane))
        s_own = jnp.sum(kn * qcol, axis=0, keepdims=True)
        s_own = jnp.where(jn <= qi, s_own - slope * (qi - jn).astype(f32), NEG)
        m = jnp.max(s_own, axis=1, keepdims=True)
        for sc in s_rows:
            m = jnp.maximum(m, jnp.max(sc, axis=1, keepdims=True))
        p_own = jnp.exp(s_own - m)
        l = jnp.sum(p_own, axis=1, keepdims=True)
        acc = jnp.zeros((HEAD_DIM, PAGE_SIZE), f32)
        for r, sc in enumerate(s_rows):
            p = jnp.exp(sc - m)
            l = l + jnp.sum(p, axis=1, keepdims=True)
            acc = acc + vbuf[slot, qi * MOBA_TOPK * ppb + r] * p
        o = jnp.sum(acc, axis=1, keepdims=True) + jnp.sum(vn * p_own, axis=1, keepdims=True)
        o_ref[:, qi:qi + 1] = o / l


def _sample_attention(sel_flat, page_table, slopes, q_t, kn_t, vn_t, ck_t, cv_t, layer):
    db, nh, dh, tn = q_t.shape
    past_len = page_table.shape[1] * PAGE_SIZE
    assert past_len % MOBA_BLOCK == 0
    npg = tn * MOBA_TOPK * (MOBA_BLOCK // PAGE_SIZE)
    blk = pl.BlockSpec((None, None, dh, tn), lambda n, *_: (n // N_HEADS, n % N_HEADS, 0, 0))
    return pl.pallas_call(
        functools.partial(_sample_attn_body, layer=layer, past_len=past_len),
        grid_spec=pltpu.PrefetchScalarGridSpec(
            num_scalar_prefetch=3,
            grid=(db * nh,),
            in_specs=[blk, blk, blk,
                      pl.BlockSpec(memory_space=pl.ANY), pl.BlockSpec(memory_space=pl.ANY)],
            out_specs=blk,
            scratch_shapes=[pltpu.VMEM((2, npg, dh, PAGE_SIZE), f32),
                            pltpu.VMEM((2, npg, dh, PAGE_SIZE), f32),
                            pltpu.SemaphoreType.DMA((2,))]),
        out_shape=jax.ShapeDtypeStruct((db, nh, dh, tn), f32),
        compiler_params=_params(("arbitrary",), 16),
        name="moba_sample_attn",
    )(sel_flat, page_table, slopes, q_t, kn_t, vn_t, ck_t, cv_t)


def _retention_tables(c_len, rows):
    lg = jnp.log(1.0 - 2.0 ** (-5.0 - jnp.arange(R_HEADS, dtype=f32)))
    idx = jnp.arange(rows, dtype=f32)
    live = idx < c_len
    diff = idx[:, None] - idx[None, :]
    decay = jnp.where((diff >= 0) & live[:, None] & live[None, :],
                      jnp.exp(lg[:, None, None] * jnp.maximum(diff, 0.0)), 0.0)
    q_dec = jnp.where(live, jnp.exp(lg[:, None] * (idx + 1.0)), 0.0)
    k_dec = jnp.where(live, jnp.exp(lg[:, None] * (c_len - 1.0 - idx)), 0.0)
    g_c = jnp.exp(lg * c_len)
    q_dec = jnp.broadcast_to(q_dec[:, :, None], (R_HEADS, rows, R_DK))
    k_dec = jnp.broadcast_to(k_dec[:, :, None], (R_HEADS, rows, R_DK))
    return decay, q_dec, k_dec, g_c


def _retention_chunk(q, k, v, g, dec, qd, kd, g_c, s_ref):
    k = k * (R_DK ** -0.5)
    vb = v.astype(bf16)
    inner = lax.dot_general(q.astype(bf16), k.astype(bf16), NT_DIMS, preferred_element_type=f32) * dec
    o = jnp.dot(inner.astype(bf16), vb, preferred_element_type=f32)
    o = o + jnp.dot((q * qd).astype(bf16), s_ref[...].astype(bf16), preferred_element_type=f32)
    s_ref[...] = s_ref[...] * g_c + lax.dot_general((k * kd).astype(bf16), vb, TN_DIMS,
                                                    preferred_element_type=f32)
    mu = jnp.mean(o, axis=-1, keepdims=True)
    oc = o - mu
    var = jnp.mean(oc * oc, axis=-1, keepdims=True)
    on = oc * lax.rsqrt(var + LN_EPS)
    return (g * (1.0 / (1.0 + jnp.exp(-g)))) * on


def _ret_prompt_body(gc_ref, q_ref, k_ref, v_ref, g_ref, dec_ref, qd_ref, kd_ref, o_ref, sfin_ref, s_ref):
    h = pl.program_id(1)
    c = pl.program_id(2)

    @pl.when(c == 0)
    def _():
        s_ref[...] = jnp.zeros(s_ref.shape, f32)

    g_c = gc_ref[h]
    C = R_CHUNK
    for cc in range(q_ref.shape[0] // C):
        r = slice(cc * C, (cc + 1) * C)
        o_ref[r, :] = _retention_chunk(q_ref[r, :], k_ref[r, :], v_ref[r, :], g_ref[r, :],
                                       dec_ref[...], qd_ref[...], kd_ref[...], g_c, s_ref)

    @pl.when(c == pl.num_programs(2) - 1)
    def _():
        sfin_ref[...] = s_ref[...]


def _retention_prompt(z, b, t):
    rb = 4 * R_CHUNK
    nrb = t // rb
    decay, q_dec, k_dec, g_c = _retention_tables(R_CHUNK, R_CHUNK)
    row = lambda bi, h, c, gc: bi * nrb + c
    tab = lambda w: pl.BlockSpec((None, R_CHUNK, w), lambda bi, h, c, gc: (h, 0, 0))
    return pl.pallas_call(
        _ret_prompt_body,
        grid_spec=pltpu.PrefetchScalarGridSpec(
            num_scalar_prefetch=1,
            grid=(b, R_HEADS, nrb),
            in_specs=[pl.BlockSpec((rb, R_DK), lambda *g: (row(*g), g[1])),
                      pl.BlockSpec((rb, R_DK), lambda *g: (row(*g), R_Q // R_DK + g[1])),
                      pl.BlockSpec((rb, R_DV), lambda *g: (row(*g), 2 * R_Q // R_DV + g[1])),
                      pl.BlockSpec((rb, R_DV), lambda *g: (row(*g), (2 * R_Q + R_V) // R_DV + g[1])),
                      tab(R_CHUNK), tab(R_DK), tab(R_DK)],
            out_specs=[pl.BlockSpec((rb, R_DV), lambda *g: (row(*g), g[1])),
                       pl.BlockSpec((None, None, R_DK, R_DV), lambda bi, h, c, gc: (bi, h, 0, 0))],
            scratch_shapes=[pltpu.VMEM((R_DK, R_DV), f32)]),
        out_shape=[jax.ShapeDtypeStruct((b * t, R_V), f32),
                   jax.ShapeDtypeStruct((b, R_HEADS, R_DK, R_DV), f32)],
        compiler_params=_params(("arbitrary", "arbitrary", "arbitrary"), 32),
        name="retention_prompt",
    )(g_c, z, z, z, z, decay, q_dec, k_dec)


def _ret_sample_body(gc_ref, q_ref, k_ref, v_ref, g_ref, s0_ref, dec_ref, qd_ref, kd_ref, o_ref, s1_ref):
    h = pl.program_id(1)
    s1_ref[...] = s0_ref[...]
    o_ref[...] = _retention_chunk(q_ref[...], k_ref[...], v_ref[...], g_ref[...],
                                  dec_ref[...], qd_ref[...], kd_ref[...], gc_ref[h], s1_ref)


def _retention_sample(z, state, db, ts):
    rows = V7X_SUBLANES
    assert ts <= rows
    zp = jnp.pad(z.reshape(db, ts, R_IN), ((0, 0), (0, rows - ts), (0, 0)))
    decay, q_dec, k_dec, g_c = _retention_tables(ts, rows)
    tab = lambda w: pl.BlockSpec((None, rows, w), lambda s, h, gc: (h, 0, 0))
    st = pl.BlockSpec((None, None, R_DK, R_DV), lambda s, h, gc: (s, h, 0, 0))
    o, s1 = pl.pallas_call(
        _ret_sample_body,
        grid_spec=pltpu.PrefetchScalarGridSpec(
            num_scalar_prefetch=1,
            grid=(db, R_HEADS),
            in_specs=[pl.BlockSpec((None, rows, R_DK), lambda s, h, gc: (s, 0, h)),
                      pl.BlockSpec((None, rows, R_DK), lambda s, h, gc: (s, 0, R_Q // R_DK + h)),
                      pl.BlockSpec((None, rows, R_DV), lambda s, h, gc: (s, 0, 2 * R_Q // R_DV + h)),
                      pl.BlockSpec((None, rows, R_DV), lambda s, h, gc: (s, 0, (2 * R_Q + R_V) // R_DV + h)),
                      st, tab(rows), tab(R_DK), tab(R_DK)],
            out_specs=[pl.BlockSpec((None, rows, R_DV), lambda s, h, gc: (s, 0, h)), st]),
        out_shape=[jax.ShapeDtypeStruct((db, rows, R_V), f32),
                   jax.ShapeDtypeStruct(state.shape, f32)],
        compiler_params=_params(("arbitrary", "arbitrary"), 16),
        name="retention_sample",
    )(g_c, zp, zp, zp, zp, state, decay, q_dec, k_dec)
    return o[:, :ts].reshape(db * ts, R_V), s1


PEER_CELLS = [(a, b) for a in range(PEER_TOPK) for b in range(PEER_TOPK) if (a + 1) * (b + 1) <= PEER_TOPK]
PEER_NCELL = len(PEER_CELLS)
PEER_NCELL_PAD = -(-PEER_NCELL // V7X_SUBLANES) * V7X_SUBLANES


def _peer_cell_tables():
    p0 = np.zeros((PEER_NCELL_PAD, PEER_TOPK), np.float32)
    p1 = np.zeros((PEER_NCELL_PAD, PEER_TOPK), np.float32)
    for x, (a, b) in enumerate(PEER_CELLS):
        p0[x, a] = 1.0
        p1[x, b] = 1.0
    return jnp.asarray(p0), jnp.asarray(p1), jnp.asarray(p0.T.copy())


def _top16(s, break_ties):
    nk = s.shape[0]
    rows = lax.broadcasted_iota(i32, s.shape, 0)
    vrow = lax.broadcasted_iota(i32, (PEER_TOPK, s.shape[1]), 0)
    rk = jnp.full(s.shape, float(PEER_TOPK), f32)
    vals = jnp.zeros((PEER_TOPK, s.shape[1]), f32)
    for r in range(PEER_TOPK):
        mx = jnp.max(s, axis=0, keepdims=True)
        pick = s == mx
        if break_ties:
            first = jnp.min(jnp.where(pick, rows, nk), axis=0, keepdims=True)
            pick = rows == first
        rk = jnp.where(pick, float(r), rk)
        s = jnp.where(pick, -jnp.inf, s)
        vals = jnp.where(vrow == r, mx, vals)
    return vals, rk


def _num_ranked(rk):
    return jnp.sum(jnp.where(rk < float(PEER_TOPK), 1.0, 0.0), axis=0, keepdims=True)


def _gather_rows(p, v):
    return jnp.dot(p, v, preferred_element_type=f32, precision=lax.Precision.HIGHEST)


def _peer_query_body(x_ref, sh_ref, sc_ref, wq_ref, o_ref):
    hmod = _modulate(x_ref[...], sh_ref[...], sc_ref[...]).astype(bf16)
    nk = PEER_HALF
    for c in range(0, wq_ref.shape[1], 4 * nk):
        r = jnp.dot(hmod, wq_ref[:, c:c + 4 * nk], preferred_element_type=f32)
        for k in range(4):
            o_ref[c // nk + k] = r[:, k * nk:(k + 1) * nk].astype(bf16)


def _peer_query(x, shift, scale, wq, tm, name):
    m, d = x.shape
    nset = 2 * PEER_HEADS
    row_of = lambda i: i
    return pl.pallas_call(
        _peer_query_body,
        grid=(m // tm,),
        in_specs=[pl.BlockSpec((tm, d), lambda i: (i, 0)), shift.spec(row_of), scale.spec(row_of),
                  pl.BlockSpec(wq.shape, lambda i: (0, 0))],
        out_specs=pl.BlockSpec((nset, tm, PEER_HALF), lambda i: (0, i, 0)),
        out_shape=jax.ShapeDtypeStruct((nset, m, PEER_HALF), bf16),
        compiler_params=_params(("arbitrary",), 32),
        name=name,
    )(x, shift.arr, scale.arr, wq)


def _pack_rows(x):
    return pltpu.bitcast(x.astype(bf16), i32)


def _unpack_rows(w):
    return pltpu.bitcast(w, bf16)


def _dup_words(x):
    hi = lax.bitcast_convert_type(x.astype(bf16).astype(f32), jnp.uint32)
    return lax.bitcast_convert_type(hi | (hi >> 16), i32)


def _peer_route_body(q_ref, keys_ref, p0_ref, p1_ref, p0t_ref, r1_ref, e1_ref, c0_ref, ez_ref,
                     v0_scr, rk0_scr, v1_scr, rk1_scr):
    G = V7X_SUBLANES
    ngrp = PEER_NCELL_PAD // G
    sub = lax.broadcasted_iota(i32, (G, q_ref.shape[1]), 0)

    def head(h, carry):
        s0 = lax.dot_general(keys_ref[2 * h], q_ref[2 * h], NT_DIMS, preferred_element_type=f32)
        s1 = lax.dot_general(keys_ref[2 * h + 1], q_ref[2 * h + 1], NT_DIMS, preferred_element_type=f32)
        v0_scr[...], rk0_scr[...] = _top16(s0, break_ties=False)
        v1_scr[...], rk1_scr[...] = _top16(s1, break_ties=False)
        extra = (jnp.abs(_num_ranked(rk0_scr[...]) - float(PEER_TOPK))
                 + jnp.abs(_num_ranked(rk1_scr[...]) - float(PEER_TOPK)))

        @pl.when(jnp.max(extra) > 0.0)
        def _():
            v0_scr[...], rk0_scr[...] = _top16(s0, break_ties=True)
            v1_scr[...], rk1_scr[...] = _top16(s1, break_ties=True)

        v0, rk0 = v0_scr[...], rk0_scr[...]
        v1, rk1 = v1_scr[...], rk1_scr[...]
        cand = _gather_rows(p0_ref[...], v0) + _gather_rows(p1_ref[...], v1)
        groups = [cand[g * G:(g + 1) * G] for g in range(ngrp)]
        counts = [jnp.zeros(groups[0].shape, f32) for _ in range(ngrp)]
        for y in range(PEER_NCELL):
            gy, ry = divmod(y, G)
            cy = groups[gy][ry:ry + 1]
            for g in range(ngrp):
                gt = jnp.where(cy > groups[g], 1.0, 0.0)
                ge = jnp.where(cy >= groups[g], 1.0, 0.0)
                if g < gy:
                    ahead = gt
                elif g > gy:
                    ahead = ge
                else:
                    ahead = jnp.where(sub > ry, ge, gt)
                counts[g] = counts[g] + ahead
        cell = lax.broadcasted_iota(i32, cand.shape, 0)
        selm = jnp.where(cell < PEER_NCELL,
                         jnp.where(jnp.concatenate(counts, axis=0) < float(PEER_TOPK), 1.0, 0.0), 0.0)
        cnt = jnp.dot(p0t_ref[...], selm, preferred_element_type=f32)
        ev0 = jnp.exp(v0 - v0[0:1])
        ev1 = jnp.exp(v1 - v1[0:1])
        z = jnp.sum(selm * _gather_rows(p0_ref[...], ev0) * _gather_rows(p1_ref[...], ev1),
                    axis=0, keepdims=True)
        c0 = jnp.zeros(s0.shape, f32)
        for a in range(PEER_TOPK):
            c0 = jnp.where(rk0 == float(a), cnt[a:a + 1], c0)
        r1_ref[h] = _pack_rows(rk1)
        e1_ref[h] = _pack_rows(jnp.exp(s1 - v1[0:1]))
        c0_ref[h] = _dup_words(c0)
        ez_ref[h] = _dup_words(0.5 * jnp.exp(s0 - v0[0:1]) / z)
        return carry

    lax.fori_loop(0, PEER_HEADS, head, 0)


def _peer_route(q, keys, tm, name):
    nset, m, half = q.shape
    p0, p1, p0t = _peer_cell_tables()
    full = lambda a: pl.BlockSpec(a.shape, lambda i: (0,) * a.ndim)
    pair = pl.BlockSpec((PEER_HEADS, PEER_NKEYS // 2, tm), lambda i: (0, 0, i))
    dup = pl.BlockSpec((PEER_HEADS, PEER_NKEYS, tm), lambda i: (0, 0, i))
    return pl.pallas_call(
        _peer_route_body,
        grid=(m // tm,),
        in_specs=[pl.BlockSpec((nset, tm, half), lambda i: (0, i, 0)), full(keys), full(p0), full(p1), full(p0t)],
        out_specs=[pair, pair, dup, dup],
        out_shape=[jax.ShapeDtypeStruct((PEER_HEADS, PEER_NKEYS // 2, m), i32)] * 2
        + [jax.ShapeDtypeStruct((PEER_HEADS, PEER_NKEYS, m), i32)] * 2,
        scratch_shapes=[pltpu.VMEM((PEER_TOPK, tm), f32), pltpu.VMEM((PEER_NKEYS, tm), f32)] * 2,
        compiler_params=_params(("arbitrary",), 40),
        name=name,
    )(q, keys, p0, p1, p0t)


PEER_TE = V7X_SUBLANES * PEER_NKEYS
PEER_TOKEN_CHUNK = 256


def _peer_expert_body(x_ref, sh_ref, sc_ref, gate_ref, g_ref, b_ref, u_ref, vt_ref,
                      r1_ref, e1_ref, c0_ref, ez_ref, o_ref, h_scr, acc_scr, hs_scr):
    e = pl.program_id(1)
    tn = x_ref.shape[0]
    nk = PEER_NKEYS
    tc = min(tn, PEER_TOKEN_CHUNK)

    @pl.when(e == 0)
    def _():
        h_scr[...] = _modulate(x_ref[...], sh_ref[...], sc_ref[...]).astype(bf16)
        acc_scr[...] = jnp.zeros(acc_scr.shape, f32)

    igrp = pl.ds(pl.multiple_of(e * V7X_SUBLANES, V7X_SUBLANES), V7X_SUBLANES)
    u = _unpack_rows(u_ref[...])
    vt = _unpack_rows(vt_ref[...])
    pre = [lax.dot_general(u, h_scr[c * tc:(c + 1) * tc, :], NT_DIMS, preferred_element_type=f32)
           for c in range(tn // tc)]
    for c in range(tn // tc):
        for il in range(PEER_TE // nk):
            for lg in range(tc // V7X_LANES):
                ln = slice(c * tc + lg * V7X_LANES, c * tc + (lg + 1) * V7X_LANES)
                terms = []
                for hh in range(PEER_HEADS):
                    crow = _unpack_rows(jnp.broadcast_to(c0_ref[hh, igrp, ln][il:il + 1], (nk // 2, V7X_LANES)))
                    zrow = _unpack_rows(jnp.broadcast_to(ez_ref[hh, igrp, ln][il:il + 1], (nk // 2, V7X_LANES)))
                    r1 = _unpack_rows(r1_ref[hh, :, ln])
                    e1 = _unpack_rows(e1_ref[hh, :, ln])
                    terms.append(jnp.where(r1 < crow, e1 * zrow, jnp.zeros((), bf16)))
                while len(terms) > 1:
                    terms = [terms[k] + terms[k + 1] for k in range(0, len(terms), 2)]
                a = pre[c][il * nk:(il + 1) * nk, lg * V7X_LANES:(lg + 1) * V7X_LANES]
                act = a * (1.0 + lax.erf(a * INV_SQRT2))
                hs_scr[il * nk:(il + 1) * nk, ln] = terms[0] * act.astype(bf16)
        cs = slice(c * tc, (c + 1) * tc)
        acc_scr[:, cs] += jnp.dot(vt, hs_scr[:, cs], preferred_element_type=f32)

    @pl.when(e == pl.num_programs(1) - 1)
    def _():
        f = acc_scr[...].T
        y = DN_ALPHA * x_ref[...] + gate_ref[...] * f
        o_ref[...] = _layer_norm(y, g_ref[...], b_ref[...])


def _peer_pack_body(u_ref, v_ref, up_ref, vp_ref):
    up_ref[...] = _pack_rows(u_ref[...])
    vp_ref[...] = _pack_rows(v_ref[...].T)


def _peer_pack_tables(u, v):
    ne, d = u.shape
    nblk = ne // PEER_TE
    return pl.pallas_call(
        _peer_pack_body,
        grid=(nblk,),
        in_specs=[pl.BlockSpec((PEER_TE, d), lambda e: (e, 0))] * 2,
        out_specs=[pl.BlockSpec((PEER_TE // 2, d), lambda e: (e, 0)),
                   pl.BlockSpec((None, d // 2, PEER_TE), lambda e: (e, 0, 0))],
        out_shape=[jax.ShapeDtypeStruct((ne // 2, d), i32),
                   jax.ShapeDtypeStruct((nblk, d // 2, PEER_TE), i32)],
        compiler_params=_params(("arbitrary",), 40),
        name="peer_pack_tables",
    )(u, v)


def _peer_experts(x, shift, scale, gate, ln_g, ln_b, u, vt, route, tm, name):
    m, d = x.shape
    ne = vt.shape[0]
    row_of = lambda i, e: i
    vec = pl.BlockSpec((1, d), lambda i, e: (0, 0))
    pair = pl.BlockSpec((PEER_HEADS, PEER_NKEYS // 2, tm), lambda i, e: (0, 0, i))
    dup = pl.BlockSpec((PEER_HEADS, PEER_NKEYS, tm), lambda i, e: (0, 0, i))
    return pl.pallas_call(
        _peer_expert_body,
        grid=(m // tm, ne),
        in_specs=[pl.BlockSpec((tm, d), lambda i, e: (i, 0)),
                  shift.spec(row_of), scale.spec(row_of), gate.spec(row_of), vec, vec,
                  pl.BlockSpec((PEER_TE // 2, d), lambda i, e: (e, 0)),
                  pl.BlockSpec((None, d // 2, PEER_TE), lambda i, e: (e, 0, 0)),
                  pair, pair, dup, dup],
        out_specs=pl.BlockSpec((tm, d), lambda i, e: (i, 0)),
        out_shape=jax.ShapeDtypeStruct((m, d), f32),
        scratch_shapes=[pltpu.VMEM((tm, d), bf16), pltpu.VMEM((d, tm), f32), pltpu.VMEM((PEER_TE, tm), bf16)],
        compiler_params=_params(("arbitrary", "arbitrary"), 48),
        name=name,
    )(x, shift.arr, scale.arr, gate.arr, ln_g.reshape(1, d), ln_b.reshape(1, d), u, vt, *route)


def kernel(x_prompt, x_sample, cache_k, cache_v, state_ret, page_table, c_prompt, c_sample, w_ada, b_ada,
           ln_g, ln_b, w_qkv_attn, w_o_attn, w_in_ret, w_o_ret, w_q_peer, keys_peer, u_peer, v_peer):
    bp, tp, d = x_prompt.shape
    bs, ts, _ = x_sample.shape
    n_p, n_s = bp * tp, bs * ts
    tm_p = 512
    depth = w_ada.shape[0]

    c_all = jnp.concatenate([c_prompt, c_sample], axis=0)
    pad = (-c_all.shape[0]) % V7X_SUBLANES
    ada = _ada(jnp.pad(c_all, ((0, pad), (0, 0))), w_ada, b_ada)

    slopes = 2.0 ** (-8.0 * jnp.arange(1, N_HEADS + 1, dtype=f32) / N_HEADS)
    ck_t = cache_k.transpose(0, 1, 3, 4, 2)
    cv_t = cache_v.transpose(0, 1, 3, 4, 2)

    yp = x_prompt.reshape(n_p, d)
    ys = x_sample.reshape(n_s, d)
    kp_l, vp_l, ks_l, vs_l, sp_l, ss_l = [], [], [], [], [], []
    for i in range(depth):
        mods_p = [_Mod(ada[i, :bp, c * d:(c + 1) * d], tm_p, tp) for c in range(6)]
        mods_s = [_Mod(ada[i, bp:bp + bs, c * d:(c + 1) * d], n_s, ts) for c in range(6)]
        j = i // 2
        if i % 2 == 0:
            w = w_qkv_attn[j].astype(bf16)
            wo = w_o_attn[j].astype(bf16)
            qh, kh, vt_h, kp_t, vp_t, kmean = _qkv_heads(yp, mods_p[0], mods_p[1], w, bp, tp, tm_p, "qkv_prompt")
            ap = _moba_prompt(qh, kh, vt_h, kmean, slopes)
            qs, _, kss, _, vss, _ = _qkv(ys, mods_s[0], mods_s[1], w, n_s, "qkv_sample")
            to_t = lambda a: a.reshape(bs, ts, N_HEADS, HEAD_DIM).transpose(0, 2, 3, 1)
            q_t = to_t(qs)
            scores = _sample_block_scores(ck_t, j, page_table, q_t)
            sel = _sample_topk(scores.reshape(bs, scores.shape[1], N_HEADS * ts))
            sel_flat = sel.reshape(bs, MOBA_TOPK, N_HEADS, ts).transpose(0, 2, 3, 1).reshape(-1)
            a_t = _sample_attention(sel_flat, page_table, slopes, q_t, to_t(kss), to_t(vss), ck_t, cv_t, j)
            a_s = a_t.transpose(0, 3, 1, 2).reshape(n_s, d)
            kp_l.append(kp_t.transpose(0, 3, 1, 2))
            vp_l.append(vp_t.transpose(0, 3, 1, 2))
            ks_l.append(kss.reshape(bs, ts, N_HEADS, HEAD_DIM))
            vs_l.append(vss.reshape(bs, ts, N_HEADS, HEAD_DIM))
        else:
            w = w_in_ret[j].astype(bf16)
            wo = w_o_ret[j].astype(bf16)
            zp = _mod_matmul(yp, mods_p[0], mods_p[1], w, tm_p, 2048, "ret_in_prompt")
            ap, s_fin = _retention_prompt(zp, bp, tp)
            zs = _mod_matmul(ys, mods_s[0], mods_s[1], w, n_s, 2048, "ret_in_sample")
            a_s, s_new = _retention_sample(zs, state_ret[j], bs, ts)
            sp_l.append(s_fin)
            ss_l.append(s_new)
        yp = _proj_res_ln(ap, wo, yp, mods_p[2], ln_g[i, 0], ln_b[i, 0], tm_p, "mixer_out_prompt")
        ys = _proj_res_ln(a_s, wo, ys, mods_s[2], ln_g[i, 0], ln_b[i, 0], n_s, "mixer_out_sample")

        wq = w_q_peer[i].astype(bf16)
        keys = keys_peer[i].reshape(2 * PEER_HEADS, PEER_NKEYS, PEER_HALF).astype(bf16)
        u, vt = _peer_pack_tables(u_peer[i], v_peer[i])
        route_p = _peer_route(_peer_query(yp, mods_p[3], mods_p[4], wq, tm_p, "peer_query_prompt"),
                              keys, 4 * V7X_LANES, "peer_route_prompt")
        yp = _peer_experts(yp, mods_p[3], mods_p[4], mods_p[5], ln_g[i, 1], ln_b[i, 1], u, vt, route_p,
                           tm_p, "peer_experts_prompt")
        route_s = _peer_route(_peer_query(ys, mods_s[3], mods_s[4], wq, n_s, "peer_query_sample"),
                              keys, n_s, "peer_route_sample")
        ys = _peer_experts(ys, mods_s[3], mods_s[4], mods_s[5], ln_g[i, 1], ln_b[i, 1], u, vt, route_s,
                           n_s, "peer_experts_sample")

    return (yp.reshape(bp, tp, d), ys.reshape(bs, ts, d),
            jnp.stack(kp_l), jnp.stack(vp_l), jnp.stack(ks_l), jnp.stack(vs_l),
            jnp.stack(sp_l), jnp.stack(ss_l))
```

```python
import functools

import jax
import jax.numpy as jnp
import numpy as np
from jax import lax
from jax.experimental import pallas as pl
from jax.experimental.pallas import tpu as pltpu

f32 = jnp.float32
bf16 = jnp.bfloat16
i32 = jnp.int32

D_MODEL = 1024
DEPTH = 2
PAGE_SIZE = 128
N_HEADS = 16
HEAD_DIM = D_MODEL // N_HEADS
MOBA_BLOCK = 256
MOBA_TOPK = 3
R_HEADS = 4
R_DK = D_MODEL // R_HEADS
R_DV = 2 * R_DK
R_Q = R_HEADS * R_DK
R_V = R_HEADS * R_DV
R_IN = 2 * R_Q + 2 * R_V
R_CHUNK = 128
PEER_HEADS = 8
PEER_NKEYS = 128
PEER_EXPERTS = PEER_NKEYS * PEER_NKEYS
PEER_HALF = 128
PEER_TOPK = 16
DN_ALPHA = (2 * DEPTH) ** 0.25
LN_EPS = 1e-5

V7X_LANES = 128
V7X_SUBLANES = 8
V7X_VMEM_BYTES = 64 * 1024 * 1024

NEG = -1e30
INV_SQRT2 = 0.7071067811865476

NT_DIMS = (((1,), (1,)), ((), ()))
TN_DIMS = (((0,), (0,)), ((), ()))


def _vmem(mib):
    assert mib * 1024 * 1024 < V7X_VMEM_BYTES
    return mib * 1024 * 1024


def _params(sem, mib):
    return pltpu.CompilerParams(dimension_semantics=sem, vmem_limit_bytes=_vmem(mib))


def _modulate(x, shift, scale):
    return x * (1.0 + scale) + shift


def _layer_norm(y, g, b):
    mu = jnp.mean(y, axis=-1, keepdims=True)
    yc = y - mu
    var = jnp.mean(yc * yc, axis=-1, keepdims=True)
    return yc * lax.rsqrt(var + LN_EPS) * g + b


class _Mod:
    def __init__(self, vec, tm, rows_per_group):
        self.tm = tm
        if rows_per_group % tm == 0:
            self.arr = vec[:, None, :]
            self.tiles_per_group = rows_per_group // tm
            self.per_row = False
        else:
            self.arr = jnp.repeat(vec, rows_per_group, axis=0)
            self.per_row = True

    def spec(self, row_of):
        d = self.arr.shape[-1]
        if self.per_row:
            return pl.BlockSpec((self.tm, d), lambda *g: (row_of(*g), 0))
        tpg = self.tiles_per_group
        return pl.BlockSpec((None, 1, d), lambda *g: (row_of(*g) // tpg, 0, 0))


def _ada_body(c_ref, w_ref, b_ref, o_ref):
    c = c_ref[...]
    a = c * (1.0 / (1.0 + jnp.exp(-c)))
    o_ref[...] = jnp.dot(a, w_ref[...], preferred_element_type=f32,
                         precision=lax.Precision.HIGHEST) + b_ref[...]


def _ada(c_all, w_ada, b_ada):
    depth, d, n6 = w_ada.shape
    rows = c_all.shape[0]
    tn = 1024
    return pl.pallas_call(
        _ada_body,
        grid=(depth, n6 // tn),
        in_specs=[pl.BlockSpec((rows, d), lambda l, j: (0, 0)),
                  pl.BlockSpec((None, d, tn), lambda l, j: (l, 0, j)),
                  pl.BlockSpec((None, 1, tn), lambda l, j: (l, 0, j))],
        out_specs=pl.BlockSpec((None, rows, tn), lambda l, j: (l, 0, j)),
        out_shape=jax.ShapeDtypeStruct((depth, rows, n6), f32),
        compiler_params=_params(("arbitrary", "arbitrary"), 24),
        name="ada",
    )(c_all, w_ada, b_ada.reshape(depth, 1, n6))


def _mod_mm_body(x_ref, sh_ref, sc_ref, w_ref, o_ref):
    h = _modulate(x_ref[...], sh_ref[...], sc_ref[...]).astype(bf16)
    n = o_ref.shape[1]
    for c in range(0, n, 512):
        o_ref[:, c:c + 512] = jnp.dot(h, w_ref[:, c:c + 512], preferred_element_type=f32)


def _mod_matmul(x, shift, scale, w, tm, tn, name):
    m, d = x.shape
    n = w.shape[1]
    row_of = lambda j, i: i
    return pl.pallas_call(
        _mod_mm_body,
        grid=(n // tn, m // tm),
        in_specs=[pl.BlockSpec((tm, d), lambda j, i: (i, 0)),
                  shift.spec(row_of), scale.spec(row_of),
                  pl.BlockSpec((d, tn), lambda j, i: (0, j))],
        out_specs=pl.BlockSpec((tm, tn), lambda j, i: (i, j)),
        out_shape=jax.ShapeDtypeStruct((m, n), f32),
        compiler_params=_params(("arbitrary", "arbitrary"), 40),
        name=name,
    )(x, shift.arr, scale.arr, w)


def _qkv_body(x_ref, sh_ref, sc_ref, w_ref, q_ref, qb_ref, k_ref, kb_ref, v_ref, vb_ref):
    h = _modulate(x_ref[...], sh_ref[...], sc_ref[...]).astype(bf16)
    d = D_MODEL
    for c in range(0, d, 512):
        q = jnp.dot(h, w_ref[:, c:c + 512], preferred_element_type=f32)
        q_ref[:, c:c + 512] = q
        qb_ref[:, c:c + 512] = (q * (HEAD_DIM ** -0.5)).astype(bf16)
        k = jnp.dot(h, w_ref[:, d + c:d + c + 512], preferred_element_type=f32)
        k_ref[:, c:c + 512] = k
        kb_ref[:, c:c + 512] = k.astype(bf16)
        v = jnp.dot(h, w_ref[:, 2 * d + c:2 * d + c + 512], preferred_element_type=f32)
        v_ref[:, c:c + 512] = v
        vb_ref[:, c:c + 512] = v.astype(bf16)


def _qkv(x, shift, scale, w, tm, name):
    m, d = x.shape
    row_of = lambda i: i
    blk = pl.BlockSpec((tm, d), lambda i: (i, 0))
    return pl.pallas_call(
        _qkv_body,
        grid=(m // tm,),
        in_specs=[blk, shift.spec(row_of), scale.spec(row_of),
                  pl.BlockSpec((d, 3 * d), lambda i: (0, 0))],
        out_specs=[blk] * 6,
        out_shape=[jax.ShapeDtypeStruct((m, d), t) for t in (f32, bf16, f32, bf16, f32, bf16)],
        compiler_params=_params(("arbitrary",), 48),
        name=name,
    )(x, shift.arr, scale.arr, w)


def _qkv_heads_body(x_ref, sh_ref, sc_ref, w_ref, qh_ref, kh_ref, vt_ref, kt_ref, vtf_ref, km_ref):
    h = _modulate(x_ref[...], sh_ref[...], sc_ref[...]).astype(bf16)
    d, dh, L = D_MODEL, HEAD_DIM, MOBA_BLOCK
    tm = x_ref.shape[0]
    cw = 512
    for c in range(0, d, cw):
        q = jnp.dot(h, w_ref[:, c:c + cw], preferred_element_type=f32) * (dh ** -0.5)
        k = jnp.dot(h, w_ref[:, d + c:d + c + cw], preferred_element_type=f32)
        v = jnp.dot(h, w_ref[:, 2 * d + c:2 * d + c + cw], preferred_element_type=f32)
        k_t = k.T
        v_t = v.T
        for blk in range(tm // L):
            km_ref[blk, :, c:c + cw] = jnp.sum(k[blk * L:(blk + 1) * L], axis=0, keepdims=True) * (1.0 / L)
        for hh in range(cw // dh):
            head = c // dh + hh
            cols = slice(hh * dh, (hh + 1) * dh)
            qh_ref[head] = q[:, cols].astype(bf16)
            kt_ref[head] = k_t[cols, :]
            vtf_ref[head] = v_t[cols, :]
            for blk in range(tm // L):
                kh_ref[head, blk] = k[blk * L:(blk + 1) * L, cols].astype(bf16)
                vt_ref[head, blk] = v_t[cols, blk * L:(blk + 1) * L].astype(bf16)


def _qkv_heads(x, shift, scale, w, b, t, tm, name):
    m, d = x.shape
    nh, dh, L = N_HEADS, HEAD_DIM, MOBA_BLOCK
    nb, tpb, bpt = t // L, t // tm, tm // L
    row_of = lambda bi, i: bi * tpb + i
    return pl.pallas_call(
        _qkv_heads_body,
        grid=(b, tpb),
        in_specs=[pl.BlockSpec((tm, d), lambda bi, i: (bi * tpb + i, 0)),
                  shift.spec(row_of), scale.spec(row_of),
                  pl.BlockSpec((d, 3 * d), lambda bi, i: (0, 0))],
        out_specs=[pl.BlockSpec((None, nh, tm, dh), lambda bi, i: (bi, 0, i, 0)),
                   pl.BlockSpec((None, nh, bpt, L, dh), lambda bi, i: (bi, 0, i, 0, 0)),
                   pl.BlockSpec((None, nh, bpt, dh, L), lambda bi, i: (bi, 0, i, 0, 0)),
                   pl.BlockSpec((None, nh, dh, tm), lambda bi, i: (bi, 0, 0, i)),
                   pl.BlockSpec((None, nh, dh, tm), lambda bi, i: (bi, 0, 0, i)),
                   pl.BlockSpec((None, bpt, 1, d), lambda bi, i: (bi, i, 0, 0))],
        out_shape=[jax.ShapeDtypeStruct((b, nh, t, dh), bf16),
                   jax.ShapeDtypeStruct((b, nh, nb, L, dh), bf16),
                   jax.ShapeDtypeStruct((b, nh, nb, dh, L), bf16),
                   jax.ShapeDtypeStruct((b, nh, dh, t), f32),
                   jax.ShapeDtypeStruct((b, nh, dh, t), f32),
                   jax.ShapeDtypeStruct((b, nb, 1, d), f32)],
        compiler_params=_params(("arbitrary", "arbitrary"), 56),
        name=name,
    )(x, shift.arr, scale.arr, w)


def _proj_ln_body(a_ref, w_ref, x_ref, gate_ref, g_ref, b_ref, o_ref):
    f = jnp.dot(a_ref[...].astype(bf16), w_ref[...], preferred_element_type=f32)
    y = DN_ALPHA * x_ref[...] + gate_ref[...] * f
    o_ref[...] = _layer_norm(y, g_ref[...], b_ref[...])


def _proj_res_ln(a, w, x, gate, ln_g, ln_b, tm, name):
    m, k = a.shape
    d = x.shape[1]
    row_of = lambda i: i
    vec = pl.BlockSpec((1, d), lambda i: (0, 0))
    return pl.pallas_call(
        _proj_ln_body,
        grid=(m // tm,),
        in_specs=[pl.BlockSpec((tm, k), lambda i: (i, 0)),
                  pl.BlockSpec((k, d), lambda i: (0, 0)),
                  pl.BlockSpec((tm, d), lambda i: (i, 0)),
                  gate.spec(row_of), vec, vec],
        out_specs=pl.BlockSpec((tm, d), lambda i: (i, 0)),
        out_shape=jax.ShapeDtypeStruct((m, d), f32),
        compiler_params=_params(("arbitrary",), 40),
        name=name,
    )(a, w, x, gate.arr, ln_g.reshape(1, d), ln_b.reshape(1, d))


def _kmean_body(k_ref, o_ref):
    o_ref[...] = jnp.sum(k_ref[...], axis=0, keepdims=True) * (1.0 / MOBA_BLOCK)


def _block_means(k):
    m, d = k.shape
    nb = m // MOBA_BLOCK
    return pl.pallas_call(
        _kmean_body,
        grid=(nb,),
        in_specs=[pl.BlockSpec((MOBA_BLOCK, d), lambda i: (i, 0))],
        out_specs=pl.BlockSpec((None, 1, d), lambda i: (i, 0, 0)),
        out_shape=jax.ShapeDtypeStruct((nb, 1, d), f32),
        compiler_params=_params(("arbitrary",), 16),
        name="moba_block_means",
    )(k)


MOBA_HEADS_PER_STEP = 8


def _moba_body(sl_ref, q_ref, k_ref, vt_ref, km_ref, o_ref, sel_ref, base_ref):
    hg = q_ref.shape[0]
    dh = q_ref.shape[2]
    h0 = pl.program_id(1) * hg
    qt = pl.program_id(2)
    L = MOBA_BLOCK
    nb = km_ref.shape[0]
    ki = lax.broadcasted_iota(i32, (L, L), 0)
    qi = lax.broadcasted_iota(i32, (L, L), 1)

    @pl.when(qt == 0)
    def _():
        rel = (qi - ki).astype(f32)
        for hh in range(hg):
            base_ref[hh] = (-sl_ref[h0 + hh]) * rel

    state = []
    for hh in range(hg):
        q = q_ref[hh]
        km = km_ref[:, 0, hh * dh:(hh + 1) * dh]
        sc = lax.dot_general(km.astype(bf16), q, NT_DIMS, preferred_element_type=f32)
        rows = lax.broadcasted_iota(i32, sc.shape, 0)
        sc = jnp.where(rows < qt, sc, -jnp.inf)
        sel = jnp.zeros(sc.shape, f32)
        for r in range(MOBA_TOPK):
            mx = jnp.max(sc, axis=0, keepdims=True)
            first = jnp.min(jnp.where(sc == mx, rows, nb), axis=0, keepdims=True)
            pick = rows == first
            slot_ok = jnp.where(qt > r, 1.0, 0.0)
            sel = jnp.maximum(sel, jnp.where(pick, slot_ok, 0.0))
            sc = jnp.where(pick, -jnp.inf, sc)
        sel_ref[hh] = sel
        s = lax.dot_general(k_ref[hh, qt], q, NT_DIMS, preferred_element_type=f32) + base_ref[hh]
        s = jnp.where(ki <= qi, s, NEG)
        m0 = jnp.max(s, axis=0, keepdims=True)
        p = jnp.exp(s - m0)
        l0 = jnp.sum(p, axis=0, keepdims=True)
        state += [m0, l0, jnp.dot(vt_ref[hh, qt], p.astype(bf16), preferred_element_type=f32)]

    def step(mb, carry):
        scores = [lax.dot_general(k_ref[hh, mb], q_ref[hh], NT_DIMS, preferred_element_type=f32)
                  for hh in range(hg)]
        out, pv = [], []
        for hh in range(hg):
            m, l, acc = carry[3 * hh:3 * hh + 3]
            cm = (qt - mb).astype(f32) * ((-sl_ref[h0 + hh]) * L)
            addrow = jnp.where(sel_ref[hh, pl.ds(mb, 1), :] > 0.0, cm, NEG)
            s = scores[hh] + base_ref[hh] + addrow
            m_new = jnp.maximum(m, jnp.max(s, axis=0, keepdims=True))
            alpha = jnp.exp(m - m_new)
            p = jnp.exp(s - m_new)
            l = l * alpha + jnp.sum(p, axis=0, keepdims=True)
            out += [m_new, l, acc * alpha]
            pv.append(p.astype(bf16))
        for hh in range(hg):
            out[3 * hh + 2] = out[3 * hh + 2] + jnp.dot(vt_ref[hh, mb], pv[hh], preferred_element_type=f32)
        return tuple(out)

    final = lax.fori_loop(0, qt, step, tuple(state))
    out_t = jnp.concatenate([final[3 * hh + 2] / final[3 * hh + 1] for hh in range(hg)], axis=0)
    o_ref[...] = out_t.T


def _moba_prompt(qh, kh, vt, km, slopes):
    b, nh, t, dh = qh.shape
    L, hg = MOBA_BLOCK, MOBA_HEADS_PER_STEP
    nb = t // L
    return pl.pallas_call(
        _moba_body,
        grid_spec=pltpu.PrefetchScalarGridSpec(
            num_scalar_prefetch=1,
            grid=(b, nh // hg, nb),
            in_specs=[pl.BlockSpec((None, hg, L, dh), lambda bi, g, i, sl: (bi, g, i, 0)),
                      pl.BlockSpec((None, hg, nb, L, dh), lambda bi, g, i, sl: (bi, g, 0, 0, 0)),
                      pl.BlockSpec((None, hg, nb, dh, L), lambda bi, g, i, sl: (bi, g, 0, 0, 0)),
                      pl.BlockSpec((None, nb, 1, hg * dh), lambda bi, g, i, sl: (bi, 0, 0, g))],
            out_specs=pl.BlockSpec((L, hg * dh), lambda bi, g, i, sl: (bi * nb + i, g)),
            scratch_shapes=[pltpu.VMEM((hg, nb, L), f32), pltpu.VMEM((hg, L, L), f32)]),
        out_shape=jax.ShapeDtypeStruct((b * t, nh * dh), f32),
        compiler_params=_params(("arbitrary", "arbitrary", "arbitrary"), 40),
        name="moba_prompt_attn",
    )(slopes, qh, kh, vt, km)


SAMPLE_PAGES_PER_STEP = 16


def _sample_scores_body(pt_ref, *refs):
    npg = SAMPLE_PAGES_PER_STEP
    k_refs, q_ref, o_ref = refs[:npg], refs[npg], refs[npg + 1]
    j = pl.program_id(1)
    ppb = MOBA_BLOCK // PAGE_SIZE
    for bb in range(npg // ppb):
        tot = k_refs[bb * ppb][...]
        for pg in range(1, ppb):
            tot = tot + k_refs[bb * ppb + pg][...]
        kmean = jnp.sum(tot, axis=-1, keepdims=True) * (1.0 / MOBA_BLOCK)
        o_ref[j * (npg // ppb) + bb] = jnp.sum(kmean * q_ref[...], axis=1)


def _sample_block_scores(ck_t, layer, page_table, q_t):
    db, n_pages = page_table.shape
    _, _, nh, dh, pg = ck_t.shape
    tn = q_t.shape[-1]
    npg = SAMPLE_PAGES_PER_STEP
    nbp = n_pages * PAGE_SIZE // MOBA_BLOCK

    def page_spec(o):
        return pl.BlockSpec((None, None, nh, dh, pg), lambda s, j, pt: (layer, pt[s, j * npg + o], 0, 0, 0))

    return pl.pallas_call(
        _sample_scores_body,
        grid_spec=pltpu.PrefetchScalarGridSpec(
            num_scalar_prefetch=1,
            grid=(db, n_pages // npg),
            in_specs=[page_spec(o) for o in range(npg)]
            + [pl.BlockSpec((None, nh, dh, tn), lambda s, j, pt: (s, 0, 0, 0))],
            out_specs=pl.BlockSpec((None, nbp, nh, tn), lambda s, j, pt: (s, 0, 0, 0))),
        out_shape=jax.ShapeDtypeStruct((db, nbp, nh, tn), f32),
        compiler_params=_params(("arbitrary", "arbitrary"), 48),
        name="moba_sample_block_scores",
    )(page_table, *([ck_t] * npg), q_t)


def _sample_topk_body(s_ref, o_ref):
    s = s_ref[...]
    nbp = s.shape[0]
    rows = lax.broadcasted_iota(i32, s.shape, 0)
    for r in range(MOBA_TOPK):
        mx = jnp.max(s, axis=0, keepdims=True)
        first = jnp.min(jnp.where(s == mx, rows, nbp), axis=0, keepdims=True)
        o_ref[r:r + 1, :] = jnp.minimum(first, nbp - 1)
        s = jnp.where(rows == first, -jnp.inf, s)


def _sample_topk(scores):
    db, nbp, c = scores.shape
    return pl.pallas_call(
        _sample_topk_body,
        grid=(db,),
        in_specs=[pl.BlockSpec((None, nbp, c), lambda s: (s, 0, 0))],
        out_specs=pl.BlockSpec((None, MOBA_TOPK, c), lambda s: (s, 0, 0)),
        out_shape=jax.ShapeDtypeStruct((db, MOBA_TOPK, c), i32),
        compiler_params=_params(("arbitrary",), 16),
        name="moba_sample_topk",
    )(scores)


SAMPLE_HEADS_PER_STEP = 2


def _sample_page_copies(sel_ref, pt_ref, ck_ref, cv_ref, kbuf, vbuf, sem, step, slot, layer, tn):
    ppb = MOBA_BLOCK // PAGE_SIZE
    hs = SAMPLE_HEADS_PER_STEP
    s = step // (N_HEADS // hs)
    h0 = (step % (N_HEADS // hs)) * hs
    out = []
    for hh in range(hs):
        for qi in range(tn):
            for sl in range(MOBA_TOPK):
                blk = sel_ref[((s * N_HEADS + h0 + hh) * tn + qi) * MOBA_TOPK + sl]
                for pg in range(ppb):
                    page = pt_ref[s, blk * ppb + pg]
                    idx = ((hh * tn + qi) * MOBA_TOPK + sl) * ppb + pg
                    out.append(pltpu.make_async_copy(ck_ref.at[layer, page, h0 + hh], kbuf.at[slot, idx],
                                                     sem.at[slot]))
                    out.append(pltpu.make_async_copy(cv_ref.at[layer, page, h0 + hh], vbuf.at[slot, idx],
                                                     sem.at[slot]))
    return out


def _sample_attn_body(sel_ref, pt_ref, sl_ref, q_ref, kn_ref, vn_ref, ck_ref, cv_ref, o_ref,
                      kbuf, vbuf, sem, *, layer, past_len):
    hs, _, tn = q_ref.shape
    ppb = MOBA_BLOCK // PAGE_SIZE
    nper = MOBA_TOPK * ppb
    n = pl.program_id(0)
    nsteps = pl.num_programs(0)
    slot = n % 2
    args = (sel_ref, pt_ref, ck_ref, cv_ref, kbuf, vbuf, sem)

    @pl.when(n == 0)
    def _():
        for c in _sample_page_copies(*args, 0, 0, layer, tn):
            c.start()

    @pl.when(n + 1 < nsteps)
    def _():
        for c in _sample_page_copies(*args, n + 1, 1 - slot, layer, tn):
            c.start()

    for c in _sample_page_copies(*args, n, slot, layer, tn):
        c.wait()

    s = n // (N_HEADS // hs)
    h0 = (n % (N_HEADS // hs)) * hs
    lane = lax.broadcasted_iota(i32, (1, PAGE_SIZE), 1).astype(f32)
    jn = lax.broadcasted_iota(i32, (1, tn), 1)
    chains = [(hh, qi) for hh in range(hs) for qi in range(tn)]
    scores = []
    for hh, qi in chains:
        slope = sl_ref[h0 + hh]
        qcol = q_ref[hh, :, qi:qi + 1] * (HEAD_DIM ** -0.5)
        t_q = past_len + qi
        s_rows = []
        for sl in range(MOBA_TOPK):
            blk = sel_ref[((s * N_HEADS + h0 + hh) * tn + qi) * MOBA_TOPK + sl]
            for pg in range(ppb):
                idx = (hh * tn + qi) * nper + sl * ppb + pg
                sc = jnp.sum(kbuf[slot, idx] * qcol, axis=0, keepdims=True)
                dist0 = (t_q - (blk * MOBA_BLOCK + pg * PAGE_SIZE)).astype(f32)
                s_rows.append(sc - slope * (dist0 - lane))
        s_own = jnp.sum(kn_ref[hh] * qcol, axis=0, keepdims=True)
        s_own = jnp.where(jn <= qi, s_own - slope * (qi - jn).astype(f32), NEG)
        scores.append((s_rows, s_own))
    maxes = []
    for s_rows, s_own in scores:
        m = jnp.max(s_own, axis=1, keepdims=True)
        for sc in s_rows:
            m = jnp.maximum(m, jnp.max(sc, axis=1, keepdims=True))
        maxes.append(m)
    for (hh, qi), (s_rows, s_own), m in zip(chains, scores, maxes):
        p_own = jnp.exp(s_own - m)
        l = jnp.sum(p_own, axis=1, keepdims=True)
        acc = jnp.zeros((HEAD_DIM, PAGE_SIZE), f32)
        for r, sc in enumerate(s_rows):
            p = jnp.exp(sc - m)
            l = l + jnp.sum(p, axis=1, keepdims=True)
            acc = acc + vbuf[slot, (hh * tn + qi) * nper + r] * p
        o = jnp.sum(acc, axis=1, keepdims=True) + jnp.sum(vn_ref[hh] * p_own, axis=1, keepdims=True)
        o_ref[hh, :, qi:qi + 1] = o / l


def _sample_attention(sel_flat, page_table, slopes, q_t, kn_t, vn_t, ck_t, cv_t, layer):
    db, nh, dh, tn = q_t.shape
    hs = SAMPLE_HEADS_PER_STEP
    past_len = page_table.shape[1] * PAGE_SIZE
    assert past_len % MOBA_BLOCK == 0
    npg = hs * tn * MOBA_TOPK * (MOBA_BLOCK // PAGE_SIZE)
    blk = pl.BlockSpec((None, hs, dh, tn), lambda n, *_: (n // (N_HEADS // hs), n % (N_HEADS // hs), 0, 0))
    return pl.pallas_call(
        functools.partial(_sample_attn_body, layer=layer, past_len=past_len),
        grid_spec=pltpu.PrefetchScalarGridSpec(
            num_scalar_prefetch=3,
            grid=(db * nh // hs,),
            in_specs=[blk, blk, blk,
                      pl.BlockSpec(memory_space=pl.ANY), pl.BlockSpec(memory_space=pl.ANY)],
            out_specs=blk,
            scratch_shapes=[pltpu.VMEM((2, npg, dh, PAGE_SIZE), f32),
                            pltpu.VMEM((2, npg, dh, PAGE_SIZE), f32),
                            pltpu.SemaphoreType.DMA((2,))]),
        out_shape=jax.ShapeDtypeStruct((db, nh, dh, tn), f32),
        compiler_params=_params(("arbitrary",), 16),
        name="moba_sample_attn",
    )(sel_flat, page_table, slopes, q_t, kn_t, vn_t, ck_t, cv_t)


def _retention_tables(c_len, rows):
    lg = jnp.log(1.0 - 2.0 ** (-5.0 - jnp.arange(R_HEADS, dtype=f32)))
    idx = jnp.arange(rows, dtype=f32)
    live = idx < c_len
    diff = idx[:, None] - idx[None, :]
    decay = jnp.where((diff >= 0) & live[:, None] & live[None, :],
                      jnp.exp(lg[:, None, None] * jnp.maximum(diff, 0.0)), 0.0)
    q_dec = jnp.where(live, jnp.exp(lg[:, None] * (idx + 1.0)), 0.0)
    k_dec = jnp.where(live, jnp.exp(lg[:, None] * (c_len - 1.0 - idx)), 0.0)
    g_c = jnp.exp(lg * c_len)
    q_dec = jnp.broadcast_to(q_dec[:, :, None], (R_HEADS, rows, R_DK))
    k_dec = jnp.broadcast_to(k_dec[:, :, None], (R_HEADS, rows, R_DK))
    return decay, q_dec, k_dec, g_c


def _retention_chunk(q, k, v, g, dec, qd, kd, g_c, s_ref):
    k = k * (R_DK ** -0.5)
    vb = v.astype(bf16)
    inner = lax.dot_general(q.astype(bf16), k.astype(bf16), NT_DIMS, preferred_element_type=f32) * dec
    o = jnp.dot(inner.astype(bf16), vb, preferred_element_type=f32)
    o = o + jnp.dot((q * qd).astype(bf16), s_ref[...].astype(bf16), preferred_element_type=f32)
    s_ref[...] = s_ref[...] * g_c + lax.dot_general((k * kd).astype(bf16), vb, TN_DIMS,
                                                    preferred_element_type=f32)
    mu = jnp.mean(o, axis=-1, keepdims=True)
    oc = o - mu
    var = jnp.mean(oc * oc, axis=-1, keepdims=True)
    on = oc * lax.rsqrt(var + LN_EPS)
    return (g * (1.0 / (1.0 + jnp.exp(-g)))) * on


def _ret_prompt_body(gc_ref, q_ref, k_ref, v_ref, g_ref, dec_ref, qd_ref, kd_ref, o_ref, sfin_ref, s_ref):
    h = pl.program_id(1)
    c = pl.program_id(2)

    @pl.when(c == 0)
    def _():
        s_ref[...] = jnp.zeros(s_ref.shape, f32)

    g_c = gc_ref[h]
    C = R_CHUNK
    for cc in range(q_ref.shape[0] // C):
        r = slice(cc * C, (cc + 1) * C)
        o_ref[r, :] = _retention_chunk(q_ref[r, :], k_ref[r, :], v_ref[r, :], g_ref[r, :],
                                       dec_ref[...], qd_ref[...], kd_ref[...], g_c, s_ref)

    @pl.when(c == pl.num_programs(2) - 1)
    def _():
        sfin_ref[...] = s_ref[...]


def _retention_prompt(z, b, t):
    rb = 4 * R_CHUNK
    nrb = t // rb
    decay, q_dec, k_dec, g_c = _retention_tables(R_CHUNK, R_CHUNK)
    row = lambda bi, h, c, gc: bi * nrb + c
    tab = lambda w: pl.BlockSpec((None, R_CHUNK, w), lambda bi, h, c, gc: (h, 0, 0))
    return pl.pallas_call(
        _ret_prompt_body,
        grid_spec=pltpu.PrefetchScalarGridSpec(
            num_scalar_prefetch=1,
            grid=(b, R_HEADS, nrb),
            in_specs=[pl.BlockSpec((rb, R_DK), lambda *g: (row(*g), g[1])),
                      pl.BlockSpec((rb, R_DK), lambda *g: (row(*g), R_Q // R_DK + g[1])),
                      pl.BlockSpec((rb, R_DV), lambda *g: (row(*g), 2 * R_Q // R_DV + g[1])),
                      pl.BlockSpec((rb, R_DV), lambda *g: (row(*g), (2 * R_Q + R_V) // R_DV + g[1])),
                      tab(R_CHUNK), tab(R_DK), tab(R_DK)],
            out_specs=[pl.BlockSpec((rb, R_DV), lambda *g: (row(*g), g[1])),
                       pl.BlockSpec((None, None, R_DK, R_DV), lambda bi, h, c, gc: (bi, h, 0, 0))],
            scratch_shapes=[pltpu.VMEM((R_DK, R_DV), f32)]),
        out_shape=[jax.ShapeDtypeStruct((b * t, R_V), f32),
                   jax.ShapeDtypeStruct((b, R_HEADS, R_DK, R_DV), f32)],
        compiler_params=_params(("arbitrary", "arbitrary", "arbitrary"), 32),
        name="retention_prompt",
    )(g_c, z, z, z, z, decay, q_dec, k_dec)


def _ret_sample_body(gc_ref, q_ref, k_ref, v_ref, g_ref, s0_ref, dec_ref, qd_ref, kd_ref, o_ref, s1_ref):
    h = pl.program_id(1)
    s1_ref[...] = s0_ref[...]
    o_ref[...] = _retention_chunk(q_ref[...], k_ref[...], v_ref[...], g_ref[...],
                                  dec_ref[...], qd_ref[...], kd_ref[...], gc_ref[h], s1_ref)


def _retention_sample(z, state, db, ts):
    rows = V7X_SUBLANES
    assert ts <= rows
    zp = jnp.pad(z.reshape(db, ts, R_IN), ((0, 0), (0, rows - ts), (0, 0)))
    decay, q_dec, k_dec, g_c = _retention_tables(ts, rows)
    tab = lambda w: pl.BlockSpec((None, rows, w), lambda s, h, gc: (h, 0, 0))
    st = pl.BlockSpec((None, None, R_DK, R_DV), lambda s, h, gc: (s, h, 0, 0))
    o, s1 = pl.pallas_call(
        _ret_sample_body,
        grid_spec=pltpu.PrefetchScalarGridSpec(
            num_scalar_prefetch=1,
            grid=(db, R_HEADS),
            in_specs=[pl.BlockSpec((None, rows, R_DK), lambda s, h, gc: (s, 0, h)),
                      pl.BlockSpec((None, rows, R_DK), lambda s, h, gc: (s, 0, R_Q // R_DK + h)),
                      pl.BlockSpec((None, rows, R_DV), lambda s, h, gc: (s, 0, 2 * R_Q // R_DV + h)),
                      pl.BlockSpec((None, rows, R_DV), lambda s, h, gc: (s, 0, (2 * R_Q + R_V) // R_DV + h)),
                      st, tab(rows), tab(R_DK), tab(R_DK)],
            out_specs=[pl.BlockSpec((None, rows, R_DV), lambda s, h, gc: (s, 0, h)), st]),
        out_shape=[jax.ShapeDtypeStruct((db, rows, R_V), f32),
                   jax.ShapeDtypeStruct(state.shape, f32)],
        compiler_params=_params(("arbitrary", "arbitrary"), 16),
        name="retention_sample",
    )(g_c, zp, zp, zp, zp, state, decay, q_dec, k_dec)
    return o[:, :ts].reshape(db * ts, R_V), s1


PEER_CELLS = [(a, b) for a in range(PEER_TOPK) for b in range(PEER_TOPK) if (a + 1) * (b + 1) <= PEER_TOPK]
PEER_NCELL = len(PEER_CELLS)
PEER_NCELL_PAD = -(-PEER_NCELL // V7X_SUBLANES) * V7X_SUBLANES


def _peer_cell_tables():
    p0 = np.zeros((PEER_NCELL_PAD, PEER_TOPK), np.float32)
    p1 = np.zeros((PEER_NCELL_PAD, PEER_TOPK), np.float32)
    for x, (a, b) in enumerate(PEER_CELLS):
        p0[x, a] = 1.0
        p1[x, b] = 1.0
    return jnp.asarray(p0), jnp.asarray(p1), jnp.asarray(p0.T.copy())


def _top16(s, break_ties):
    nk = s.shape[0]
    rows = lax.broadcasted_iota(i32, s.shape, 0)
    vrow = lax.broadcasted_iota(i32, (PEER_TOPK, s.shape[1]), 0)
    rk = jnp.full(s.shape, float(PEER_TOPK), f32)
    vals = jnp.zeros((PEER_TOPK, s.shape[1]), f32)
    for r in range(PEER_TOPK):
        mx = jnp.max(s, axis=0, keepdims=True)
        pick = s == mx
        if break_ties:
            first = jnp.min(jnp.where(pick, rows, nk), axis=0, keepdims=True)
            pick = rows == first
        rk = jnp.where(pick, float(r), rk)
        s = jnp.where(pick, -jnp.inf, s)
        vals = jnp.where(vrow == r, mx, vals)
    return vals, rk


def _num_ranked(rk):
    return jnp.sum(jnp.where(rk < float(PEER_TOPK), 1.0, 0.0), axis=0, keepdims=True)


def _gather_rows(p, v):
    return jnp.dot(p, v, preferred_element_type=f32, precision=lax.Precision.HIGHEST)


def _peer_query_body(x_ref, sh_ref, sc_ref, wq_ref, o_ref):
    hmod = _modulate(x_ref[...], sh_ref[...], sc_ref[...]).astype(bf16)
    nk = PEER_HALF
    for c in range(0, wq_ref.shape[1], 4 * nk):
        r = jnp.dot(hmod, wq_ref[:, c:c + 4 * nk], preferred_element_type=f32)
        for k in range(4):
            o_ref[c // nk + k] = r[:, k * nk:(k + 1) * nk].astype(bf16)


def _peer_query(x, shift, scale, wq, tm, name):
    m, d = x.shape
    nset = 2 * PEER_HEADS
    row_of = lambda i: i
    return pl.pallas_call(
        _peer_query_body,
        grid=(m // tm,),
        in_specs=[pl.BlockSpec((tm, d), lambda i: (i, 0)), shift.spec(row_of), scale.spec(row_of),
                  pl.BlockSpec(wq.shape, lambda i: (0, 0))],
        out_specs=pl.BlockSpec((nset, tm, PEER_HALF), lambda i: (0, i, 0)),
        out_shape=jax.ShapeDtypeStruct((nset, m, PEER_HALF), bf16),
        compiler_params=_params(("arbitrary",), 32),
        name=name,
    )(x, shift.arr, scale.arr, wq)


def _pack_rows(x):
    return pltpu.bitcast(x.astype(bf16), i32)


def _unpack_rows(w):
    return pltpu.bitcast(w, bf16)


def _dup_words(x):
    hi = lax.bitcast_convert_type(x.astype(bf16).astype(f32), jnp.uint32)
    return lax.bitcast_convert_type(hi | (hi >> 16), i32)


def _peer_route_body(q_ref, keys_ref, p0_ref, p1_ref, p0t_ref, r1_ref, e1_ref, c0_ref, ez_ref,
                     v0_scr, rk0_scr, v1_scr, rk1_scr):
    G = V7X_SUBLANES
    ngrp = PEER_NCELL_PAD // G
    sub = lax.broadcasted_iota(i32, (G, q_ref.shape[1]), 0)

    def head(h, carry):
        s0 = lax.dot_general(keys_ref[2 * h], q_ref[2 * h], NT_DIMS, preferred_element_type=f32)
        s1 = lax.dot_general(keys_ref[2 * h + 1], q_ref[2 * h + 1], NT_DIMS, preferred_element_type=f32)
        v0_scr[...], rk0_scr[...] = _top16(s0, break_ties=False)
        v1_scr[...], rk1_scr[...] = _top16(s1, break_ties=False)
        extra = (jnp.abs(_num_ranked(rk0_scr[...]) - float(PEER_TOPK))
                 + jnp.abs(_num_ranked(rk1_scr[...]) - float(PEER_TOPK)))

        @pl.when(jnp.max(extra) > 0.0)
        def _():
            v0_scr[...], rk0_scr[...] = _top16(s0, break_ties=True)
            v1_scr[...], rk1_scr[...] = _top16(s1, break_ties=True)

        v0, rk0 = v0_scr[...], rk0_scr[...]
        v1, rk1 = v1_scr[...], rk1_scr[...]
        cand = _gather_rows(p0_ref[...], v0) + _gather_rows(p1_ref[...], v1)
        groups = [cand[g * G:(g + 1) * G] for g in range(ngrp)]
        counts = [jnp.zeros(groups[0].shape, f32) for _ in range(ngrp)]
        for y in range(PEER_NCELL):
            gy, ry = divmod(y, G)
            cy = groups[gy][ry:ry + 1]
            for g in range(ngrp):
                gt = jnp.where(cy > groups[g], 1.0, 0.0)
                ge = jnp.where(cy >= groups[g], 1.0, 0.0)
                if g < gy:
                    ahead = gt
                elif g > gy:
                    ahead = ge
                else:
                    ahead = jnp.where(sub > ry, ge, gt)
                counts[g] = counts[g] + ahead
        cell = lax.broadcasted_iota(i32, cand.shape, 0)
        selm = jnp.where(cell < PEER_NCELL,
                         jnp.where(jnp.concatenate(counts, axis=0) < float(PEER_TOPK), 1.0, 0.0), 0.0)
        cnt = jnp.dot(p0t_ref[...], selm, preferred_element_type=f32)
        ev0 = jnp.exp(v0 - v0[0:1])
        ev1 = jnp.exp(v1 - v1[0:1])
        z = jnp.sum(selm * _gather_rows(p0_ref[...], ev0) * _gather_rows(p1_ref[...], ev1),
                    axis=0, keepdims=True)
        c0 = jnp.zeros(s0.shape, f32)
        for a in range(PEER_TOPK):
            c0 = jnp.where(rk0 == float(a), cnt[a:a + 1], c0)
        r1_ref[h] = _pack_rows(rk1)
        e1_ref[h] = _pack_rows(jnp.exp(s1 - v1[0:1]))
        c0_ref[h] = _dup_words(c0)
        ez_ref[h] = _dup_words(0.5 * jnp.exp(s0 - v0[0:1]) / z)
        return carry

    lax.fori_loop(0, PEER_HEADS, head, 0)


def _peer_route(q, keys, tm, name):
    nset, m, half = q.shape
    p0, p1, p0t = _peer_cell_tables()
    full = lambda a: pl.BlockSpec(a.shape, lambda i: (0,) * a.ndim)
    pair = pl.BlockSpec((PEER_HEADS, PEER_NKEYS // 2, tm), lambda i: (0, 0, i))
    dup = pl.BlockSpec((PEER_HEADS, PEER_NKEYS, tm), lambda i: (0, 0, i))
    return pl.pallas_call(
        _peer_route_body,
        grid=(m // tm,),
        in_specs=[pl.BlockSpec((nset, tm, half), lambda i: (0, i, 0)), full(keys), full(p0), full(p1), full(p0t)],
        out_specs=[pair, pair, dup, dup],
        out_shape=[jax.ShapeDtypeStruct((PEER_HEADS, PEER_NKEYS // 2, m), i32)] * 2
        + [jax.ShapeDtypeStruct((PEER_HEADS, PEER_NKEYS, m), i32)] * 2,
        scratch_shapes=[pltpu.VMEM((PEER_TOPK, tm), f32), pltpu.VMEM((PEER_NKEYS, tm), f32)] * 2,
        compiler_params=_params(("arbitrary",), 40),
        name=name,
    )(q, keys, p0, p1, p0t)


PEER_TE = V7X_SUBLANES * PEER_NKEYS
PEER_TOKEN_CHUNK = 256


def _peer_expert_body(x_ref, sh_ref, sc_ref, gate_ref, g_ref, b_ref, u_ref, vt_ref,
                      r1_ref, e1_ref, c0_ref, ez_ref, o_ref, h_scr, acc_scr, hs_scr):
    e = pl.program_id(1)
    tn = x_ref.shape[0]
    nk = PEER_NKEYS
    tc = min(tn, PEER_TOKEN_CHUNK)

    @pl.when(e == 0)
    def _():
        h_scr[...] = _modulate(x_ref[...], sh_ref[...], sc_ref[...]).astype(bf16)
        acc_scr[...] = jnp.zeros(acc_scr.shape, f32)

    igrp = pl.ds(pl.multiple_of(e * V7X_SUBLANES, V7X_SUBLANES), V7X_SUBLANES)
    u = _unpack_rows(u_ref[...])
    vt = _unpack_rows(vt_ref[...])
    pre = [lax.dot_general(u, h_scr[c * tc:(c + 1) * tc, :], NT_DIMS, preferred_element_type=f32)
           for c in range(tn // tc)]
    for c in range(tn // tc):
        for il in range(PEER_TE // nk):
            for lg in range(tc // V7X_LANES):
                ln = slice(c * tc + lg * V7X_LANES, c * tc + (lg + 1) * V7X_LANES)
                terms = []
                for hh in range(PEER_HEADS):
                    crow = _unpack_rows(jnp.broadcast_to(c0_ref[hh, igrp, ln][il:il + 1], (nk // 2, V7X_LANES)))
                    zrow = _unpack_rows(jnp.broadcast_to(ez_ref[hh, igrp, ln][il:il + 1], (nk // 2, V7X_LANES)))
                    r1 = _unpack_rows(r1_ref[hh, :, ln])
                    e1 = _unpack_rows(e1_ref[hh, :, ln])
                    terms.append(jnp.where(r1 < crow, e1 * zrow, jnp.zeros((), bf16)))
                while len(terms) > 1:
                    terms = [terms[k] + terms[k + 1] for k in range(0, len(terms), 2)]
                a = pre[c][il * nk:(il + 1) * nk, lg * V7X_LANES:(lg + 1) * V7X_LANES]
                act = a * (1.0 + lax.erf(a * INV_SQRT2))
                hs_scr[il * nk:(il + 1) * nk, ln] = terms[0] * act.astype(bf16)
        cs = slice(c * tc, (c + 1) * tc)
        acc_scr[:, cs] += jnp.dot(vt, hs_scr[:, cs], preferred_element_type=f32)

    @pl.when(e == pl.num_programs(1) - 1)
    def _():
        f = acc_scr[...].T
        y = DN_ALPHA * x_ref[...] + gate_ref[...] * f
        o_ref[...] = _layer_norm(y, g_ref[...], b_ref[...])


def _peer_pack_body(u_ref, v_ref, up_ref, vp_ref):
    up_ref[...] = _pack_rows(u_ref[...])
    vp_ref[...] = _pack_rows(v_ref[...].T)


def _peer_pack_tables(u, v, layer):
    _, ne, d = u.shape
    nblk = ne // PEER_TE
    return pl.pallas_call(
        _peer_pack_body,
        grid=(nblk,),
        in_specs=[pl.BlockSpec((None, PEER_TE, d), lambda e: (layer, e, 0))] * 2,
        out_specs=[pl.BlockSpec((PEER_TE // 2, d), lambda e: (e, 0)),
                   pl.BlockSpec((None, d // 2, PEER_TE), lambda e: (e, 0, 0))],
        out_shape=[jax.ShapeDtypeStruct((ne // 2, d), i32),
                   jax.ShapeDtypeStruct((nblk, d // 2, PEER_TE), i32)],
        compiler_params=_params(("arbitrary",), 40),
        name="peer_pack_tables",
    )(u, v)


def _peer_experts(x, shift, scale, gate, ln_g, ln_b, u, vt, route, tm, name):
    m, d = x.shape
    ne = vt.shape[0]
    row_of = lambda i, e: i
    vec = pl.BlockSpec((1, d), lambda i, e: (0, 0))
    pair = pl.BlockSpec((PEER_HEADS, PEER_NKEYS // 2, tm), lambda i, e: (0, 0, i))
    dup = pl.BlockSpec((PEER_HEADS, PEER_NKEYS, tm), lambda i, e: (0, 0, i))
    return pl.pallas_call(
        _peer_expert_body,
        grid=(m // tm, ne),
        in_specs=[pl.BlockSpec((tm, d), lambda i, e: (i, 0)),
                  shift.spec(row_of), scale.spec(row_of), gate.spec(row_of), vec, vec,
                  pl.BlockSpec((PEER_TE // 2, d), lambda i, e: (e, 0)),
                  pl.BlockSpec((None, d // 2, PEER_TE), lambda i, e: (e, 0, 0)),
                  pair, pair, dup, dup],
        out_specs=pl.BlockSpec((tm, d), lambda i, e: (i, 0)),
        out_shape=jax.ShapeDtypeStruct((m, d), f32),
        scratch_shapes=[pltpu.VMEM((tm, d), bf16), pltpu.VMEM((d, tm), f32), pltpu.VMEM((PEER_TE, tm), bf16)],
        compiler_params=_params(("arbitrary", "arbitrary"), 48),
        name=name,
    )(x, shift.arr, scale.arr, gate.arr, ln_g.reshape(1, d), ln_b.reshape(1, d), u, vt, *route)


def kernel(x_prompt, x_sample, cache_k, cache_v, state_ret, page_table, c_prompt, c_sample, w_ada, b_ada,
           ln_g, ln_b, w_qkv_attn, w_o_attn, w_in_ret, w_o_ret, w_q_peer, keys_peer, u_peer, v_peer):
    bp, tp, d = x_prompt.shape
    bs, ts, _ = x_sample.shape
    n_p, n_s = bp * tp, bs * ts
    tm_p = 512
    depth = w_ada.shape[0]

    c_all = jnp.concatenate([c_prompt, c_sample], axis=0)
    pad = (-c_all.shape[0]) % V7X_SUBLANES
    ada = _ada(jnp.pad(c_all, ((0, pad), (0, 0))), w_ada, b_ada)

    slopes = 2.0 ** (-8.0 * jnp.arange(1, N_HEADS + 1, dtype=f32) / N_HEADS)
    ck_t = cache_k.transpose(0, 1, 3, 4, 2)
    cv_t = cache_v.transpose(0, 1, 3, 4, 2)

    yp = x_prompt.reshape(n_p, d)
    ys = x_sample.reshape(n_s, d)
    kp_l, vp_l, ks_l, vs_l, sp_l, ss_l = [], [], [], [], [], []
    for i in range(depth):
        mods_p = [_Mod(ada[i, :bp, c * d:(c + 1) * d], tm_p, tp) for c in range(6)]
        mods_s = [_Mod(ada[i, bp:bp + bs, c * d:(c + 1) * d], n_s, ts) for c in range(6)]
        j = i // 2
        if i % 2 == 0:
            w = w_qkv_attn[j].astype(bf16)
            wo = w_o_attn[j].astype(bf16)
            qh, kh, vt_h, kp_t, vp_t, kmean = _qkv_heads(yp, mods_p[0], mods_p[1], w, bp, tp, tm_p, "qkv_prompt")
            ap = _moba_prompt(qh, kh, vt_h, kmean, slopes)
            qs, _, kss, _, vss, _ = _qkv(ys, mods_s[0], mods_s[1], w, n_s, "qkv_sample")
            to_t = lambda a: a.reshape(bs, ts, N_HEADS, HEAD_DIM).transpose(0, 2, 3, 1)
            q_t = to_t(qs)
            scores = _sample_block_scores(ck_t, j, page_table, q_t)
            sel = _sample_topk(scores.reshape(bs, scores.shape[1], N_HEADS * ts))
            sel_flat = sel.reshape(bs, MOBA_TOPK, N_HEADS, ts).transpose(0, 2, 3, 1).reshape(-1)
            a_t = _sample_attention(sel_flat, page_table, slopes, q_t, to_t(kss), to_t(vss), ck_t, cv_t, j)
            a_s = a_t.transpose(0, 3, 1, 2).reshape(n_s, d)
            kp_l.append(kp_t.transpose(0, 3, 1, 2))
            vp_l.append(vp_t.transpose(0, 3, 1, 2))
            ks_l.append(kss.reshape(bs, ts, N_HEADS, HEAD_DIM))
            vs_l.append(vss.reshape(bs, ts, N_HEADS, HEAD_DIM))
        else:
            w = w_in_ret[j].astype(bf16)
            wo = w_o_ret[j].astype(bf16)
            zp = _mod_matmul(yp, mods_p[0], mods_p[1], w, tm_p, 2048, "ret_in_prompt")
            ap, s_fin = _retention_prompt(zp, bp, tp)
            zs = _mod_matmul(ys, mods_s[0], mods_s[1], w, n_s, 2048, "ret_in_sample")
            a_s, s_new = _retention_sample(zs, state_ret[j], bs, ts)
            sp_l.append(s_fin)
            ss_l.append(s_new)
        yp = _proj_res_ln(ap, wo, yp, mods_p[2], ln_g[i, 0], ln_b[i, 0], tm_p, "mixer_out_prompt")
        ys = _proj_res_ln(a_s, wo, ys, mods_s[2], ln_g[i, 0], ln_b[i, 0], n_s, "mixer_out_sample")

        wq = w_q_peer[i].astype(bf16)
        keys = keys_peer[i].reshape(2 * PEER_HEADS, PEER_NKEYS, PEER_HALF).astype(bf16)
        u, vt = _peer_pack_tables(u_peer, v_peer, i)
        route_p = _peer_route(_peer_query(yp, mods_p[3], mods_p[4], wq, tm_p, "peer_query_prompt"),
                              keys, 4 * V7X_LANES, "peer_route_prompt")
        yp = _peer_experts(yp, mods_p[3], mods_p[4], mods_p[5], ln_g[i, 1], ln_b[i, 1], u, vt, route_p,
                           tm_p, "peer_experts_prompt")
        route_s = _peer_route(_peer_query(ys, mods_s[3], mods_s[4], wq, n_s, "peer_query_sample"),
                              keys, n_s, "peer_route_sample")
        ys = _peer_experts(ys, mods_s[3], mods_s[4], mods_s[5], ln_g[i, 1], ln_b[i, 1], u, vt, route_s,
                           n_s, "peer_experts_sample")

    return (yp.reshape(bp, tp, d), ys.reshape(bs, ts, d),
            jnp.stack(kp_l), jnp.stack(vp_l), jnp.stack(ks_l), jnp.stack(vs_l),
            jnp.stack(sp_l), jnp.stack(ss_l))
```
